```python
import jax, jax.numpy as jnp
from jax import lax
import numpy as np

D_MODEL = 2048
BATCH = 4
SEQ = 2048
DEPTH = 4

GRID_W = 64
CTX_LEN = 256
N_MIXERS = 2
N_HEADS = 16
HEAD_DIM = D_MODEL // N_HEADS
WIN_ROWS_MAX = 8
WIN_COLS = 16
LRU_BLOCK = 256
LRU_WIDTH = -(-4 * D_MODEL // (3 * LRU_BLOCK)) * LRU_BLOCK
N_LRU_BLOCKS = LRU_WIDTH // LRU_BLOCK
CONV_WIDTH = 4
LRU_C = 8.0
D_FF = 4 * D_MODEL
N_MOD = 6
EPS = 1e-6
N_LRU_LAYERS = (DEPTH + 1) // 2
N_NA_LAYERS = DEPTH // 2

kernel_name = 'hybrid_rglru_natten_dit_block'


def _rmsnorm(x, g):
    xf = x.astype(jnp.float32)
    xf = xf * lax.rsqrt(jnp.mean(xf * xf, axis=-1, keepdims=True) + EPS)
    return xf.astype(x.dtype) * g


def _modulate(h, shift, scale):
    return h * (1.0 + scale) + shift


def _sq_relu_mlp(h, w1, w2):
    return jnp.square(jax.nn.relu(h @ w1)) @ w2


def _centred_dwconv(u, w, b):
    left = CONV_WIDTH // 2
    right = CONV_WIDTH - 1 - left
    y = lax.conv_general_dilated(u, w[:, None, :].astype(u.dtype), window_strides=(1,),
                                 padding=[(left, right)], dimension_numbers=('NWC', 'WIO', 'NWC'),
                                 feature_group_count=u.shape[-1])
    return y + b


def _block_diag(u, w, b):
    bsz, length, _ = u.shape
    ub = u.reshape(bsz, length, N_LRU_BLOCKS, LRU_BLOCK)
    return (jnp.einsum('blnj,njk->blnk', ub, w) + b).reshape(bsz, length, LRU_WIDTH)


def _rglru_coeffs(u, lam, w_a, b_a, w_x, b_x):
    uf = u.astype(jnp.float32)
    r = jax.nn.sigmoid(_block_diag(uf, w_a.astype(jnp.float32), b_a.astype(jnp.float32)))
    i_g = jax.nn.sigmoid(_block_diag(uf, w_x.astype(jnp.float32), b_x.astype(jnp.float32)))
    log_a = -LRU_C * r * jax.nn.softplus(-lam.astype(jnp.float32))
    a = jnp.exp(log_a)
    b = jnp.sqrt(-jnp.expm1(2.0 * log_a)) * (i_g * uf)
    return a, b


def _linear_scan(a, b, h0):
    def combine(l, r):
        return (l[0] * r[0], r[0] * l[1] + r[1])
    a_cum, h = lax.associative_scan(combine, (a, b), axis=1)
    return h + a_cum * h0[:, None, :]


def _flip(t, direction):
    return t[:, ::-1] if direction == 1 else t


def _rglru_mixer(h, hc, w_in, conv_w, conv_b, lam, w_a, b_a, w_x, b_x, w_out, need_ctx):
    gate, u = jnp.split(h @ w_in, 2, axis=-1)
    if need_ctx:
        gate_c, u_c = jnp.split(hc @ w_in, 2, axis=-1)
    else:
        u_c = hc @ w_in[:, LRU_WIDTH:]
    u = _centred_dwconv(u, conv_w, conv_b)
    u_c = _centred_dwconv(u_c, conv_w, conv_b)
    ys, ys_c = [], []
    for d in range(2):
        a_c, b_c = _rglru_coeffs(_flip(u_c, d), lam[d], w_a[d], b_a[d], w_x[d], b_x[d])
        h_c = _linear_scan(a_c, b_c, jnp.zeros_like(a_c[:, 0]))
        a_l, b_l = _rglru_coeffs(_flip(u, d), lam[d], w_a[d], b_a[d], w_x[d], b_x[d])
        h_l = _linear_scan(a_l, b_l, h_c[:, -1])
        ys.append(_flip(h_l, d))
        if need_ctx:
            ys_c.append(_flip(h_c, d))
    y = (ys[0] + ys[1]).astype(h.dtype)
    out = (jax.nn.gelu(gate) * y) @ w_out
    out_c = None
    if need_ctx:
        y_c = (ys_c[0] + ys_c[1]).astype(hc.dtype)
        out_c = (jax.nn.gelu(gate_c) * y_c) @ w_out
    return out, out_c


def _na_mixer(h, hc, w_qkv, rpb, w_o, need_ctx):
    bsz, length, _ = h.shape
    rows = length // GRID_W
    kh = min(WIN_ROWS_MAX, rows)
    scale = HEAD_DIM ** -0.5
    q, k, v = jnp.split(h @ w_qkv, 3, axis=-1)
    q = q.reshape(bsz, rows, GRID_W, N_HEADS, HEAD_DIM)
    k = k.reshape(bsz, rows, GRID_W, N_HEADS, HEAD_DIM)
    v = v.reshape(bsz, rows, GRID_W, N_HEADS, HEAD_DIM)
    if need_ctx:
        qc, kc, vc = jnp.split(hc @ w_qkv, 3, axis=-1)
        qc = qc.reshape(bsz, CTX_LEN, N_HEADS, HEAD_DIM)
    else:
        kc, vc = jnp.split(hc @ w_qkv[:, D_MODEL:], 2, axis=-1)
    kc = kc.reshape(bsz, CTX_LEN, N_HEADS, HEAD_DIM)
    vc = vc.reshape(bsz, CTX_LEN, N_HEADS, HEAD_DIM)

    cols = jnp.arange(GRID_W)
    col_start = jnp.clip(cols - WIN_COLS // 2, 0, GRID_W - WIN_COLS)
    col_idx = col_start[:, None] + jnp.arange(WIN_COLS)[None, :]
    col_off = col_idx - cols[:, None] + (WIN_COLS - 1)
    bias_cols = rpb[:, :, col_off].astype(jnp.float32)

    def row_block(r):
        rs = jnp.clip(r - kh // 2, 0, rows - kh)
        q_r = lax.dynamic_index_in_dim(q, r, axis=1, keepdims=False)
        k_win = lax.dynamic_slice_in_dim(k, rs, kh, axis=1)[:, :, col_idx]
        v_win = lax.dynamic_slice_in_dim(v, rs, kh, axis=1)[:, :, col_idx]
        row_off = rs + jnp.arange(kh) - r + (WIN_ROWS_MAX - 1)
        bias = jnp.transpose(bias_cols[:, row_off], (0, 2, 1, 3))
        s_loc = jnp.einsum('bwhd,bawkhd->bhwak', q_r, k_win).astype(jnp.float32) * scale + bias[None]
        s_ctx = jnp.einsum('bwhd,bchd->bhwc', q_r, kc).astype(jnp.float32) * scale
        s = jnp.concatenate([s_loc.reshape(bsz, N_HEADS, GRID_W, kh * WIN_COLS), s_ctx], axis=-1)
        p = jax.nn.softmax(s, axis=-1).astype(v.dtype)
        p_loc = p[..., :kh * WIN_COLS].reshape(bsz, N_HEADS, GRID_W, kh, WIN_COLS)
        p_ctx = p[..., kh * WIN_COLS:]
        return (jnp.einsum('bhwak,bawkhd->bwhd', p_loc, v_win)
                + jnp.einsum('bhwc,bchd->bwhd', p_ctx, vc))

    o = lax.map(row_block, jnp.arange(rows))
    o = jnp.transpose(o, (1, 0, 2, 3, 4)).reshape(bsz, length, D_MODEL)
    out = o @ w_o
    out_c = None
    if need_ctx:
        s_c = jnp.einsum('bqhd,bkhd->bhqk', qc, kc).astype(jnp.float32) * scale
        p_c = jax.nn.softmax(s_c, axis=-1).astype(vc.dtype)
        o_c = jnp.einsum('bhqk,bkhd->bqhd', p_c, vc).reshape(bsz, CTX_LEN, D_MODEL)
        out_c = o_c @ w_o
    return out, out_c


def setup_inputs(seed: int = 0) -> dict:
    key = jax.random.key(seed)
    ks = jax.random.split(key, 24)
    f32 = jnp.float32
    nrm = lambda k, shape, s: jax.random.normal(k, shape, f32) * s
    u = jax.random.uniform(ks[12], (N_LRU_LAYERS, 2, LRU_WIDTH), f32, minval=0.9, maxval=0.999)
    s_lam = u ** (1.0 / LRU_C)
    lru_lambda = jnp.log(s_lam) - jnp.log1p(-s_lam)
    return {
        'x': nrm(ks[0], (BATCH, SEQ, D_MODEL), 1.0),
        'c': nrm(ks[1], (BATCH, D_MODEL), 1.0),
        'ctx': nrm(ks[2], (BATCH, CTX_LEN, D_MODEL), 1.0),
        'c_ctx': nrm(ks[3], (D_MODEL,), 1.0),
        'ada_w': nrm(ks[4], (DEPTH, D_MODEL, N_MOD * D_MODEL), 0.5 * D_MODEL ** -0.5),
        'ada_b': nrm(ks[5], (DEPTH, N_MOD * D_MODEL), 0.02),
        'norm1_g': 1.0 + nrm(ks[6], (DEPTH, D_MODEL), 0.02),
        'norm2_g': 1.0 + nrm(ks[7], (DEPTH, D_MODEL), 0.02),
        'mlp_w1': nrm(ks[8], (DEPTH, D_MODEL, D_FF), D_MODEL ** -0.5),
        'mlp_w2': nrm(ks[9], (DEPTH, D_FF, D_MODEL), D_FF ** -0.5),
        'lru_w_in': nrm(ks[10], (N_LRU_LAYERS, D_MODEL, 2 * LRU_WIDTH), D_MODEL ** -0.5),
        'lru_conv_w': nrm(ks[11], (N_LRU_LAYERS, CONV_WIDTH, LRU_WIDTH), CONV_WIDTH ** -0.5),
        'lru_conv_b': nrm(ks[13], (N_LRU_LAYERS, LRU_WIDTH), 0.02),
        'lru_lambda': lru_lambda,
        'lru_wa': nrm(ks[14], (N_LRU_LAYERS, 2, N_LRU_BLOCKS, LRU_BLOCK, LRU_BLOCK), LRU_BLOCK ** -0.5),
        'lru_ba': nrm(ks[15], (N_LRU_LAYERS, 2, N_LRU_BLOCKS, LRU_BLOCK), 0.02),
        'lru_wx': nrm(ks[16], (N_LRU_LAYERS, 2, N_LRU_BLOCKS, LRU_BLOCK, LRU_BLOCK), LRU_BLOCK ** -0.5),
        'lru_bx': nrm(ks[17], (N_LRU_LAYERS, 2, N_LRU_BLOCKS, LRU_BLOCK), 0.02),
        'lru_w_out': nrm(ks[18], (N_LRU_LAYERS, LRU_WIDTH, D_MODEL), LRU_WIDTH ** -0.5),
        'na_w_qkv': nrm(ks[19], (N_NA_LAYERS, D_MODEL, 3 * D_MODEL), D_MODEL ** -0.5),
        'na_rpb': nrm(ks[20], (N_NA_LAYERS, N_HEADS, 2 * WIN_ROWS_MAX - 1, 2 * WIN_COLS - 1), 0.1),
        'na_w_o': nrm(ks[21], (N_NA_LAYERS, D_MODEL, D_MODEL), D_MODEL ** -0.5),
        'final_g': 1.0 + nrm(ks[22], (D_MODEL,), 0.02),
    }


def reference(x, c, ctx, c_ctx, ada_w, ada_b, norm1_g, norm2_g, mlp_w1, mlp_w2,
              lru_w_in, lru_conv_w, lru_conv_b, lru_lambda, lru_wa, lru_ba, lru_wx, lru_bx, lru_w_out,
              na_w_qkv, na_rpb, na_w_o, final_g):
    xc = ctx
    silu_c = jax.nn.silu(c)
    silu_cc = jax.nn.silu(c_ctx)
    for i in range(DEPTH):
        need_ctx = i < DEPTH - 1
        mods = silu_c @ ada_w[i] + ada_b[i]
        sh1, sc1, gt1, sh2, sc2, gt2 = [m[:, None, :] for m in jnp.split(mods, N_MOD, axis=-1)]
        if need_ctx:
            csh1, csc1, cgt1, csh2, csc2, cgt2 = jnp.split(silu_cc @ ada_w[i] + ada_b[i], N_MOD, axis=-1)
        else:
            csh1, csc1 = jnp.split(silu_cc @ ada_w[i][:, :2 * D_MODEL] + ada_b[i][:2 * D_MODEL], 2, axis=-1)
        h = _modulate(_rmsnorm(x, norm1_g[i]), sh1, sc1)
        hc = _modulate(_rmsnorm(xc, norm1_g[i]), csh1, csc1)
        j = i // N_MIXERS
        if i % N_MIXERS == 0:
            out, out_c = _rglru_mixer(h, hc, lru_w_in[j], lru_conv_w[j], lru_conv_b[j], lru_lambda[j],
                                      lru_wa[j], lru_ba[j], lru_wx[j], lru_bx[j], lru_w_out[j], need_ctx)
        else:
            out, out_c = _na_mixer(h, hc, na_w_qkv[j], na_rpb[j], na_w_o[j], need_ctx)
        x = x + gt1 * out
        h2 = _modulate(_rmsnorm(x, norm2_g[i]), sh2, sc2)
        x = x + gt2 * _sq_relu_mlp(h2, mlp_w1[i], mlp_w2[i])
        if need_ctx:
            xc = xc + cgt1 * out_c
            hc2 = _modulate(_rmsnorm(xc, norm2_g[i]), csh2, csc2)
            xc = xc + cgt2 * _sq_relu_mlp(hc2, mlp_w1[i], mlp_w2[i])
    return _rmsnorm(x, final_g)
```

```python
import functools

import numpy as np
import jax
import jax.numpy as jnp
from jax import lax
from jax.experimental import pallas as pl
from jax.experimental.pallas import tpu as pltpu

GRID_W = 64
WIN_ROWS = 8
WIN_COLS = 16
HEAD_DIM = 128
LRU_BLOCK = 256
CONV_WIDTH = 4
LRU_C = 8.0
N_MOD = 6
EPS = 1e-6

LANES = 128
CTX_ROWS = 256
NORM_CHUNK = 32
QROWS = 4
KROWS = QROWS + WIN_ROWS
SEG_PITCH = LRU_BLOCK + 8
NEG = -1e30
VMEM_LIMIT = 56 * 1024 * 1024

F32 = jnp.float32
BF16 = jnp.bfloat16


def _cparams(*sem):
    return pltpu.CompilerParams(dimension_semantics=sem, vmem_limit_bytes=VMEM_LIMIT)


def _mod_spec(layer, k, row_fn, width, col_fn=None):
    if col_fn is None:
        return pl.BlockSpec((None, None, None, 1, width),
                            lambda *g: (layer, row_fn(*g), k, 0, 0))
    return pl.BlockSpec((None, None, None, 1, width),
                        lambda *g: (layer, row_fn(*g), k, 0, col_fn(*g)))


def _ada_kernel(c_ref, w_ref, b_ref, o_ref):
    c = c_ref[...]
    s = (c * jax.nn.sigmoid(c)).astype(BF16)
    o_ref[...] = jnp.dot(s, w_ref[...].astype(BF16), preferred_element_type=F32) + b_ref[...]


def _ada_mods(c8, ada_w, ada_b, tn=1024):
    depth, d, n = ada_w.shape
    tn = int(np.gcd(tn, n))
    return pl.pallas_call(
        _ada_kernel,
        grid=(depth, n // tn),
        in_specs=[pl.BlockSpec((8, d), lambda l, j: (0, 0)),
                  pl.BlockSpec((None, d, tn), lambda l, j: (l, 0, j)),
                  pl.BlockSpec((None, 1, tn), lambda l, j: (l, 0, j))],
        out_specs=pl.BlockSpec((None, 8, tn), lambda l, j: (l, 0, j)),
        out_shape=jax.ShapeDtypeStruct((depth, 8, n), F32),
        compiler_params=_cparams("parallel", "parallel"),
        name="ada_mods",
    )(c8, ada_w, ada_b.reshape(depth, 1, n))


def _norm_mod_rows(x_ref, g_ref, sh_ref, sc_ref, csh_ref, csc_ref, first, hn_ref):
    tm = x_ref.shape[0]
    g = g_ref[...]
    sh, sc = sh_ref[...], sc_ref[...]
    csh, csc = csh_ref[...], csc_ref[...]

    def body(r, carry):
        rows = pl.ds(pl.multiple_of(r * NORM_CHUNK, NORM_CHUNK), NORM_CHUNK)
        xs = x_ref[rows, :]
        ms = jnp.mean(xs * xs, axis=-1, keepdims=True)
        xn = xs * lax.rsqrt(ms + EPS) * g
        is_ctx = jnp.logical_and(first, r * NORM_CHUNK < CTX_ROWS)
        scale = jnp.where(is_ctx, csc, sc)
        shift = jnp.where(is_ctx, csh, sh)
        hn_ref[rows, :] = (xn * (1.0 + scale) + shift).astype(hn_ref.dtype)
        return carry

    lax.fori_loop(0, tm // NORM_CHUNK, body, 0)


def _mm_norm_kernel(x_ref, g_ref, sh_ref, sc_ref, csh_ref, csc_ref, w_ref, o_ref, hn_ref, *, tpb):
    i = pl.program_id(0)

    @pl.when(pl.program_id(1) == 0)
    def _():
        _norm_mod_rows(x_ref, g_ref, sh_ref, sc_ref, csh_ref, csc_ref, (i % tpb) == 0, hn_ref)

    o_ref[...] = jnp.dot(hn_ref[...], w_ref[...], preferred_element_type=F32).astype(o_ref.dtype)


def _mm_norm(x, g, mods, layer, w, out_dtype, *, tm, tn, tpb):
    m, d = x.shape
    n = w.shape[1]
    row = lambda i, j: i // tpb
    ctx = lambda i, j: 4
    return pl.pallas_call(
        functools.partial(_mm_norm_kernel, tpb=tpb),
        grid=(m // tm, n // tn),
        in_specs=[pl.BlockSpec((tm, d), lambda i, j: (i, 0)),
                  pl.BlockSpec((1, d), lambda i, j: (0, 0)),
                  _mod_spec(layer, 0, row, d), _mod_spec(layer, 1, row, d),
                  _mod_spec(layer, 0, ctx, d), _mod_spec(layer, 1, ctx, d),
                  pl.BlockSpec((d, tn), lambda i, j: (0, j))],
        out_specs=pl.BlockSpec((tm, tn), lambda i, j: (i, j)),
        out_shape=jax.ShapeDtypeStruct((m, n), out_dtype),
        scratch_shapes=[pltpu.VMEM((tm, d), BF16)],
        compiler_params=_cparams("parallel", "arbitrary"),
        name="mm_norm",
    )(x, g, mods, mods, mods, mods, w)


def _mm_res_kernel(a_ref, w_ref, r_ref, gt_ref, cgt_ref, o_ref, *, tpb):
    first = (pl.program_id(0) % tpb) == 0
    acc = jnp.dot(a_ref[...], w_ref[...], preferred_element_type=F32)
    g = gt_ref[...]
    g0 = jnp.where(first, cgt_ref[...], g)
    o_ref[:CTX_ROWS, :] = r_ref[:CTX_ROWS, :] + g0 * acc[:CTX_ROWS]
    o_ref[CTX_ROWS:, :] = r_ref[CTX_ROWS:, :] + g * acc[CTX_ROWS:]


def _mm_res(a, w, res, mods, layer, *, tm, tn, tpb):
    m, k = a.shape
    n = w.shape[1]
    return pl.pallas_call(
        functools.partial(_mm_res_kernel, tpb=tpb),
        grid=(m // tm, n // tn),
        in_specs=[pl.BlockSpec((tm, k), lambda i, j: (i, 0)),
                  pl.BlockSpec((k, tn), lambda i, j: (0, j)),
                  pl.BlockSpec((tm, tn), lambda i, j: (i, j)),
                  _mod_spec(layer, 2, lambda i, j: i // tpb, tn, lambda i, j: j),
                  _mod_spec(layer, 2, lambda i, j: 4, tn, lambda i, j: j)],
        out_specs=pl.BlockSpec((tm, tn), lambda i, j: (i, j)),
        out_shape=jax.ShapeDtypeStruct((m, n), F32),
        compiler_params=_cparams("parallel", "parallel"),
        name="mm_res",
    )(a, w, res, mods, mods)


def _mlp_kernel(x_ref, g_ref, sh_ref, sc_ref, gt_ref, csh_ref, csc_ref, cgt_ref, w1_ref, w2_ref,
                o_ref, hn_ref, *, tpb):
    i = pl.program_id(0)
    f = pl.program_id(1)
    first = (i % tpb) == 0

    @pl.when(f == 0)
    def _():
        _norm_mod_rows(x_ref, g_ref, sh_ref, sc_ref, csh_ref, csc_ref, first, hn_ref)

    a = jnp.dot(hn_ref[...], w1_ref[...], preferred_element_type=F32)
    a = jnp.square(jnp.maximum(a, 0.0)).astype(BF16)
    p = jnp.dot(a, w2_ref[...], preferred_element_type=F32)

    @pl.when(f == 0)
    def _():
        o_ref[...] = p

    @pl.when(f > 0)
    def _():
        o_ref[...] += p

    @pl.when(f == pl.num_programs(1) - 1)
    def _():
        g = gt_ref[...]
        g0 = jnp.where(first, cgt_ref[...], g)
        o_ref[:CTX_ROWS, :] = x_ref[:CTX_ROWS, :] + g0 * o_ref[:CTX_ROWS, :]
        o_ref[CTX_ROWS:, :] = x_ref[CTX_ROWS:, :] + g * o_ref[CTX_ROWS:, :]


def _mlp(x, g, mods, layer, w1, w2, *, tm, tf, tpb):
    m, d = x.shape
    dff = w1.shape[1]
    row = lambda i, f: i // tpb
    ctx = lambda i, f: 4
    return pl.pallas_call(
        functools.partial(_mlp_kernel, tpb=tpb),
        grid=(m // tm, dff // tf),
        in_specs=[pl.BlockSpec((tm, d), lambda i, f: (i, 0)),
                  pl.BlockSpec((1, d), lambda i, f: (0, 0)),
                  _mod_spec(layer, 3, row, d), _mod_spec(layer, 4, row, d), _mod_spec(layer, 5, row, d),
                  _mod_spec(layer, 3, ctx, d), _mod_spec(layer, 4, ctx, d), _mod_spec(layer, 5, ctx, d),
                  pl.BlockSpec((d, tf), lambda i, f: (0, f)),
                  pl.BlockSpec((tf, d), lambda i, f: (f, 0))],
        out_specs=pl.BlockSpec((tm, d), lambda i, f: (i, 0)),
        out_shape=jax.ShapeDtypeStruct((m, d), F32),
        scratch_shapes=[pltpu.VMEM((tm, d), BF16)],
        compiler_params=_cparams("parallel", "arbitrary"),
        name="mlp",
    )(x, g, mods, mods, mods, mods, mods, mods, w1, w2)


def _sigmoid(x):
    return 0.5 * jnp.tanh(0.5 * x) + 0.5


def _gelu_tanh(x):
    return 0.5 * x * (1.0 + jnp.tanh(0.7978845608028654 * (x + 0.044715 * (x * x * x))))


def _lru_kernel(gate_ref, u_ref, cw_ref, cb_ref, lam_ref, wa_ref, ba_ref, wx_ref, bx_ref, z_ref,
                upad, wcat, a_s, b_s, *, nseg):
    seg = LRU_BLOCK
    nh = LRU_BLOCK // LANES
    lat0 = CTX_ROWS + 16
    zeros8 = jnp.zeros((8, LANES), F32)
    for l in range(nh):
        lanes = slice(l * LANES, (l + 1) * LANES)
        upad[l, 0:8, :] = zeros8
        upad[l, 8:8 + CTX_ROWS, :] = u_ref[0:CTX_ROWS, lanes]
        upad[l, 8 + CTX_ROWS:lat0, :] = zeros8
        upad[l, lat0:lat0 + (nseg - 1) * seg, :] = u_ref[CTX_ROWS:, lanes]
        upad[l, lat0 + (nseg - 1) * seg:lat0 + (nseg - 1) * seg + 8, :] = zeros8
    for d in range(2):
        wcat[:, (2 * d) * seg:(2 * d + 1) * seg] = wa_ref[d].astype(BF16)
        wcat[:, (2 * d + 1) * seg:(2 * d + 2) * seg] = wx_ref[d].astype(BF16)

    lam = lam_ref[...]
    nlam = -lam
    coef = -LRU_C * (jnp.maximum(nlam, 0.0) + jnp.log1p(jnp.exp(-jnp.abs(nlam))))
    cw = cw_ref[...]
    cb = cb_ref[...]
    ba = ba_ref[...]
    bx = bx_ref[...]

    def coeff_body(s, carry):
        base = s * seg + jnp.where(s == 0, 8, 16)
        ucs = []
        for l in range(nh):
            lanes = slice(l * LANES, (l + 1) * LANES)
            acc = cb[:, lanes]
            for k in range(CONV_WIDTH):
                acc = acc + cw[k:k + 1, lanes] * upad[l, pl.ds(base + (k - CONV_WIDTH // 2), seg), :]
            ucs.append(acc)
        uc = jnp.concatenate(ucs, axis=-1)
        gts = jnp.dot(uc.astype(BF16), wcat[...], preferred_element_type=F32)
        row0 = pl.multiple_of(s * SEG_PITCH, 8)
        for d in range(2):
            for l in range(nh):
                lanes = slice(l * LANES, (l + 1) * LANES)
                ga = gts[:, 2 * d * seg + l * LANES:2 * d * seg + (l + 1) * LANES]
                gx = gts[:, (2 * d + 1) * seg + l * LANES:(2 * d + 1) * seg + (l + 1) * LANES]
                r = _sigmoid(ga + ba[d, :, lanes])
                ig = _sigmoid(gx + bx[d, :, lanes])
                log_a = r * coef[d:d + 1, lanes]
                a = jnp.exp(log_a)
                bb = jnp.sqrt(-jnp.tanh(log_a) * (a * a + 1.0)) * (ig * ucs[l])
                a_s[d, l, pl.ds(row0, seg), :] = a
                b_s[d, l, pl.ds(row0, seg), :] = bb
        return carry

    lax.fori_loop(0, nseg, coeff_body, 0)

    def ld(ref, d, l, j):
        v8 = ref[d, l, pl.ds(j, 8, stride=SEG_PITCH), :]
        v1 = ref[d, l, pl.ds(j + 8 * SEG_PITCH, 1), :]
        return v8, v1

    def st(ref, d, l, j, v8, v1):
        ref[d, l, pl.ds(j, 8, stride=SEG_PITCH), :] = v8
        ref[d, l, pl.ds(j + 8 * SEG_PITCH, 1), :] = v1

    def scan_body(t, carry):
        out = []
        for d in range(2):
            j = t if d == 0 else seg - 1 - t
            for l in range(nh):
                h8, h1, p8, p1 = carry[d * nh + l]
                a8, a1 = ld(a_s, d, l, j)
                b8, b1 = ld(b_s, d, l, j)
                h8 = a8 * h8 + b8
                h1 = a1 * h1 + b1
                p8 = a8 * p8
                p1 = a1 * p1
                st(a_s, d, l, j, p8, p1)
                st(b_s, d, l, j, h8, h1)
                out.append((h8, h1, p8, p1))
        return tuple(out)

    init = tuple((jnp.zeros((8, LANES), F32), jnp.zeros((1, LANES), F32),
                  jnp.ones((8, LANES), F32), jnp.ones((1, LANES), F32)) for _ in range(2 * nh))
    fin = lax.fori_loop(0, seg, scan_body, init)

    def seg_row(v8, v1, s):
        return v1 if s == 8 else v8[s:s + 1, :]

    carries = []
    for d in range(2):
        per_l = []
        for l in range(nh):
            h8, h1, p8, p1 = fin[d * nh + l]
            order = list(range(nseg)) if d == 0 else [0] + list(range(nseg - 1, 0, -1))
            c = jnp.zeros((1, LANES), F32)
            cs = {}
            for s in order:
                cs[s] = c
                c = seg_row(h8, h1, s) + seg_row(p8, p1, s) * c
            per_l.append(cs)
        carries.append(per_l)

    for s in range(nseg):
        rows = slice(s * seg, (s + 1) * seg)
        srows = slice(s * SEG_PITCH, s * SEG_PITCH + seg)
        for l in range(nh):
            lanes = slice(l * LANES, (l + 1) * LANES)
            y = (b_s[0, l, srows, :] + a_s[0, l, srows, :] * carries[0][l][s]
                 + b_s[1, l, srows, :] + a_s[1, l, srows, :] * carries[1][l][s])
            z_ref[rows, lanes] = (_gelu_tanh(gate_ref[rows, lanes]) * y).astype(z_ref.dtype)


def _lru_core(gu, conv_w, conv_b, lam, wa, ba, wx, bx, *, batch, t):
    m = gu.shape[0]
    width = gu.shape[1] // 2
    nb = width // LRU_BLOCK
    nseg = t // LRU_BLOCK
    assert nseg == 9, "scan kernel keeps 8 + 1 time segments on sublanes"
    nh = LRU_BLOCK // LANES
    srows = nseg * SEG_PITCH
    return pl.pallas_call(
        functools.partial(_lru_kernel, nseg=nseg),
        grid=(batch, nb),
        in_specs=[pl.BlockSpec((t, LRU_BLOCK), lambda b, n: (b, n)),
                  pl.BlockSpec((t, LRU_BLOCK), lambda b, n: (b, nb + n)),
                  pl.BlockSpec((CONV_WIDTH, LRU_BLOCK), lambda b, n: (0, n)),
                  pl.BlockSpec((1, LRU_BLOCK), lambda b, n: (0, n)),
                  pl.BlockSpec((2, LRU_BLOCK), lambda b, n: (0, n)),
                  pl.BlockSpec((2, None, LRU_BLOCK, LRU_BLOCK), lambda b, n: (0, n, 0, 0)),
                  pl.BlockSpec((2, None, 1, LRU_BLOCK), lambda b, n: (0, n, 0, 0)),
                  pl.BlockSpec((2, None, LRU_BLOCK, LRU_BLOCK), lambda b, n: (0, n, 0, 0)),
                  pl.BlockSpec((2, None, 1, LRU_BLOCK), lambda b, n: (0, n, 0, 0))],
        out_specs=pl.BlockSpec((t, LRU_BLOCK), lambda b, n: (b, n)),
        out_shape=jax.ShapeDtypeStruct((m, width), BF16),
        scratch_shapes=[pltpu.VMEM((nh, t + 24, LANES), F32),
                        pltpu.VMEM((LRU_BLOCK, 4 * LRU_BLOCK), BF16),
                        pltpu.VMEM((2, nh, srows, LANES), F32),
                        pltpu.VMEM((2, nh, srows, LANES), F32)],
        compiler_params=_cparams("parallel", "parallel"),
        name="lru_core",
    )(gu, gu, conv_w, conv_b.reshape(1, width), lam, wa, ba.reshape(2, nb, 1, LRU_BLOCK),
      wx, bx.reshape(2, nb, 1, LRU_BLOCK))


def _softmax_rows(parts):
    m = parts[0].max(axis=-1, keepdims=True)
    for p in parts[1:]:
        m = jnp.maximum(m, p.max(axis=-1, keepdims=True))
    es = [jnp.exp(p - m) for p in parts]
    tot = es[0].sum(axis=-1, keepdims=True)
    for e in es[1:]:
        tot = tot + e.sum(axis=-1, keepdims=True)
    inv = 1.0 / tot
    return [(e * inv).astype(BF16) for e in es]


def _na_kernel(q_ref, k_ref, v_ref, bias_ref, o_ref, *, n_qblocks):
    j = pl.program_id(2)
    scale = HEAD_DIM ** -0.5
    nt = (((1,), (1,)), ((), ()))
    q = q_ref[...]
    kc = k_ref[0:CTX_ROWS, :]
    vc = v_ref[0:CTX_ROWS, :]
    s_ctx = lax.dot_general(q, kc, nt, preferred_element_type=F32) * scale

    @pl.when(j == 0)
    def _():
        (p,) = _softmax_rows([s_ctx])
        o_ref[...] = jnp.dot(p, vc, preferred_element_type=F32).astype(o_ref.dtype)

    @pl.when(j > 0)
    def _():
        jj = j - 1
        n_rows = n_qblocks * QROWS
        ws = jnp.clip(jj * QROWS - WIN_ROWS // 2, 0, n_rows - KROWS)
        start = pl.multiple_of(CTX_ROWS + ws * GRID_W, CTX_ROWS)
        var = jnp.where(jj == 0, 0, jnp.where(jj == n_qblocks - 1, 2, 1))
        kw = k_ref[pl.ds(start, KROWS * GRID_W), :]
        vw = v_ref[pl.ds(start, KROWS * GRID_W), :]
        s_loc = lax.dot_general(q, kw, nt, preferred_element_type=F32) * scale + bias_ref[var]
        p_loc, p_ctx = _softmax_rows([s_loc, s_ctx])
        o = jnp.dot(p_loc, vw, preferred_element_type=F32) + jnp.dot(p_ctx, vc, preferred_element_type=F32)
        o_ref[...] = o.astype(o_ref.dtype)


def _na_bias_tables(rpb, n_rows):
    n_heads = rpb.shape[0]
    qc = np.arange(GRID_W)[:, None]
    kc = np.arange(GRID_W)[None, :]
    cs = np.clip(qc - WIN_COLS // 2, 0, GRID_W - WIN_COLS)
    col_ok = (kc >= cs) & (kc < cs + WIN_COLS)
    col_off = np.where(col_ok, kc - qc + WIN_COLS - 1, 0)
    onehot = (np.arange(2 * WIN_COLS - 1)[:, None, None] == col_off[None]) & col_ok[None]
    onehot = jnp.asarray(onehot.astype(np.float32))
    t2 = jnp.einsum('hrm,mqk->hrqk', rpb.astype(F32), onehot, precision=lax.Precision.HIGHEST)
    t2 = jnp.where(jnp.asarray(col_ok)[None, None], t2, NEG)
    n_dr = 2 * WIN_ROWS - 1
    t2 = jnp.concatenate([t2, jnp.full((n_heads, 1, GRID_W, GRID_W), NEG, F32)], axis=1)
    idx = np.full((3, QROWS, KROWS), n_dr, np.int32)
    n_qblocks = n_rows // QROWS
    for v, jj in enumerate((0, 1, n_qblocks - 1)):
        r0 = jj * QROWS
        ws = int(np.clip(r0 - WIN_ROWS // 2, 0, n_rows - KROWS))
        for a in range(QROWS):
            qr = r0 + a
            rs = int(np.clip(qr - WIN_ROWS // 2, 0, n_rows - WIN_ROWS))
            for kr in range(KROWS):
                kra = ws + kr
                if rs <= kra < rs + WIN_ROWS:
                    idx[v, a, kr] = kra - qr + WIN_ROWS - 1
    full = t2[:, idx]
    full = jnp.transpose(full, (0, 1, 2, 4, 3, 5))
    return full.reshape(n_heads, 3, QROWS * GRID_W, KROWS * GRID_W)


def _na_attention(qkv, bias, *, batch, t):
    m = qkv.shape[0]
    d = qkv.shape[1] // 3
    n_heads = d // HEAD_DIM
    qb = QROWS * GRID_W
    n_qblocks = (t - CTX_ROWS) // qb
    bpb = t // qb
    return pl.pallas_call(
        functools.partial(_na_kernel, n_qblocks=n_qblocks),
        grid=(n_heads, batch, bpb),
        in_specs=[pl.BlockSpec((qb, HEAD_DIM), lambda h, b, j: (b * bpb + j, h)),
                  pl.BlockSpec((t, HEAD_DIM), lambda h, b, j: (b, n_heads + h)),
                  pl.BlockSpec((t, HEAD_DIM), lambda h, b, j: (b, 2 * n_heads + h)),
                  pl.BlockSpec((None, 3, qb, KROWS * GRID_W), lambda h, b, j: (h, 0, 0, 0))],
        out_specs=pl.BlockSpec((qb, HEAD_DIM), lambda h, b, j: (b * bpb + j, h)),
        out_shape=jax.ShapeDtypeStruct((m, d), BF16),
        compiler_params=_cparams("parallel", "parallel", "parallel"),
        name="na_attn",
    )(qkv, qkv, qkv, bias)


def _final_norm_kernel(x_ref, g_ref, o_ref):
    g = g_ref[...]

    def body(r, carry):
        rows = pl.ds(pl.multiple_of(r * NORM_CHUNK, NORM_CHUNK), NORM_CHUNK)
        xs = x_ref[rows, :]
        ms = jnp.mean(xs * xs, axis=-1, keepdims=True)
        o_ref[rows, :] = xs * lax.rsqrt(ms + EPS) * g
        return carry

    lax.fori_loop(0, x_ref.shape[0] // NORM_CHUNK, body, 0)


def _final_norm(x, g, *, batch, t, seq):
    d = x.shape[1]
    nblk = seq // CTX_ROWS
    return pl.pallas_call(
        _final_norm_kernel,
        grid=(batch, nblk),
        in_specs=[pl.BlockSpec((None, CTX_ROWS, d), lambda b, j: (b, 1 + j, 0)),
                  pl.BlockSpec((1, d), lambda b, j: (0, 0))],
        out_specs=pl.BlockSpec((None, CTX_ROWS, d), lambda b, j: (b, j, 0)),
        out_shape=jax.ShapeDtypeStruct((batch, seq, d), F32),
        compiler_params=_cparams("parallel", "parallel"),
        name="final_norm",
    )(x.reshape(batch, t, d), g)


def kernel(x, c, ctx, c_ctx, ada_w, ada_b, norm1_g, norm2_g, mlp_w1, mlp_w2, lru_w_in, lru_conv_w, lru_conv_b, lru_lambda, lru_wa, lru_ba, lru_wx, lru_bx, lru_w_out, na_w_qkv, na_rpb, na_w_o, final_g):
    batch, seq, d = x.shape
    ctx_len = ctx.shape[1]
    depth = ada_w.shape[0]
    assert ctx_len == CTX_ROWS and batch <= 4 and seq % (QROWS * GRID_W) == 0
    t = ctx_len + seq
    tpb = 3
    tm = t // tpb
    assert tm % CTX_ROWS == 0
    tn = min(512, d)

    xs = jnp.concatenate([ctx, x], axis=1).reshape(batch * t, d)
    c8 = jnp.zeros((8, d), F32).at[:batch].set(c).at[4].set(c_ctx)
    mods = _ada_mods(c8, ada_w, ada_b).reshape(depth, 8, N_MOD, 1, d)

    for i in range(depth):
        j = i // 2
        g1 = norm1_g[i].reshape(1, d)
        g2 = norm2_g[i].reshape(1, d)
        if i % 2 == 0:
            gu = _mm_norm(xs, g1, mods, i, lru_w_in[j].astype(BF16), F32, tm=tm, tn=tn, tpb=tpb)
            z = _lru_core(gu, lru_conv_w[j], lru_conv_b[j], lru_lambda[j], lru_wa[j], lru_ba[j],
                          lru_wx[j], lru_bx[j], batch=batch, t=t)
            xs = _mm_res(z, lru_w_out[j].astype(BF16), xs, mods, i, tm=tm, tn=tn, tpb=tpb)
        else:
            qkv = _mm_norm(xs, g1, mods, i, na_w_qkv[j].astype(BF16), BF16, tm=tm, tn=tn, tpb=tpb)
            bias = _na_bias_tables(na_rpb[j], seq // GRID_W)
            o = _na_attention(qkv, bias, batch=batch, t=t)
            xs = _mm_res(o, na_w_o[j].astype(BF16), xs, mods, i, tm=tm, tn=tn, tpb=tpb)
        xs = _mlp(xs, g2, mods, i, mlp_w1[i].astype(BF16), mlp_w2[i].astype(BF16),
                  tm=tm, tf=min(512, mlp_w1.shape[2]), tpb=tpb)

    return _final_norm(xs, final_g.reshape(1, d), batch=batch, t=t, seq=seq)
```

```python
import functools
import math

import numpy as np
import jax
import jax.numpy as jnp
from jax import lax
from jax.experimental import pallas as pl
from jax.experimental.pallas import tpu as pltpu

GRID_W = 64
WIN_ROWS = 8
WIN_COLS = 16
HEAD_DIM = 128
LRU_BLOCK = 256
CONV_WIDTH = 4
LRU_C = 8.0
N_MOD = 6
EPS = 1e-6

LANES = 128
CTX_ROWS = 256
NORM_CHUNK = 32
QROWS = 4
KROWS = QROWS + WIN_ROWS
SEG_PITCH = LRU_BLOCK + 8
SCAN_UNROLL = 4
NEG = -1e30
VMEM_LIMIT = 56 * 1024 * 1024

F32 = jnp.float32
BF16 = jnp.bfloat16


def _cparams(*sem):
    return pltpu.CompilerParams(dimension_semantics=sem, vmem_limit_bytes=VMEM_LIMIT)


def _mod_spec(layer, k, row_fn, width, col_fn=None):
    if col_fn is None:
        return pl.BlockSpec((None, None, None, 1, width),
                            lambda *g: (layer, row_fn(*g), k, 0, 0))
    return pl.BlockSpec((None, None, None, 1, width),
                        lambda *g: (layer, row_fn(*g), k, 0, col_fn(*g)))


def _gain_spec(layer, d):
    return pl.BlockSpec((None, 1, d), lambda *g: (layer, 0, 0))


def _ada_kernel(c_ref, w_ref, b_ref, o_ref):
    c = c_ref[...]
    s = (c * jax.nn.sigmoid(c)).astype(BF16)
    o_ref[...] = jnp.dot(s, w_ref[...].astype(BF16), preferred_element_type=F32) + b_ref[...]


def _ada_mods(c8, ada_w, ada_b, tn=1024):
    depth, d, n = ada_w.shape
    tn = math.gcd(tn, n)
    return pl.pallas_call(
        _ada_kernel,
        grid=(depth, n // tn),
        in_specs=[pl.BlockSpec((8, d), lambda l, j: (0, 0)),
                  pl.BlockSpec((None, d, tn), lambda l, j: (l, 0, j)),
                  pl.BlockSpec((None, 1, tn), lambda l, j: (l, 0, j))],
        out_specs=pl.BlockSpec((None, 8, tn), lambda l, j: (l, 0, j)),
        out_shape=jax.ShapeDtypeStruct((depth, 8, n), F32),
        compiler_params=_cparams("parallel", "parallel"),
        name="ada_mods",
    )(c8, ada_w, ada_b.reshape(depth, 1, n))


def _norm_mod_rows(x_ref, g_ref, sh_ref, sc_ref, csh_ref, csc_ref, first, hn_ref):
    tm = x_ref.shape[0]
    g = g_ref[...]
    sh, sc = sh_ref[...], sc_ref[...]
    csh, csc = csh_ref[...], csc_ref[...]

    def body(r, carry):
        rows = pl.ds(pl.multiple_of(r * NORM_CHUNK, NORM_CHUNK), NORM_CHUNK)
        xs = x_ref[rows, :]
        ms = jnp.mean(xs * xs, axis=-1, keepdims=True)
        xn = xs * lax.rsqrt(ms + EPS) * g
        is_ctx = jnp.logical_and(first, r * NORM_CHUNK < CTX_ROWS)
        scale = jnp.where(is_ctx, csc, sc)
        shift = jnp.where(is_ctx, csh, sh)
        hn_ref[rows, :] = (xn * (1.0 + scale) + shift).astype(hn_ref.dtype)
        return carry

    lax.fori_loop(0, tm // NORM_CHUNK, body, 0)


def _mm_norm_kernel(x_ref, g_ref, sh_ref, sc_ref, csh_ref, csc_ref, w_ref, o_ref, hn_ref, *, tpb):
    i = pl.program_id(0)

    @pl.when(pl.program_id(1) == 0)
    def _():
        _norm_mod_rows(x_ref, g_ref, sh_ref, sc_ref, csh_ref, csc_ref, (i % tpb) == 0, hn_ref)

    o_ref[...] = jnp.dot(hn_ref[...], w_ref[...], preferred_element_type=F32).astype(o_ref.dtype)


def _mm_norm(x, gains, mods, layer, w, wl, out_dtype, *, tm, tn, tpb):
    m, d = x.shape
    n = w.shape[2]
    row = lambda i, j: i // tpb
    ctx = lambda i, j: 4
    return pl.pallas_call(
        functools.partial(_mm_norm_kernel, tpb=tpb),
        grid=(m // tm, n // tn),
        in_specs=[pl.BlockSpec((tm, d), lambda i, j: (i, 0)),
                  _gain_spec(layer, d),
                  _mod_spec(layer, 0, row, d), _mod_spec(layer, 1, row, d),
                  _mod_spec(layer, 0, ctx, d), _mod_spec(layer, 1, ctx, d),
                  pl.BlockSpec((None, d, tn), lambda i, j: (wl, 0, j))],
        out_specs=pl.BlockSpec((tm, tn), lambda i, j: (i, j)),
        out_shape=jax.ShapeDtypeStruct((m, n), out_dtype),
        scratch_shapes=[pltpu.VMEM((tm, d), BF16)],
        compiler_params=_cparams("parallel", "arbitrary"),
        name="mm_norm",
    )(x, gains, mods, mods, mods, mods, w)


def _mm_res_kernel(a_ref, w_ref, r_ref, gt_ref, cgt_ref, o_ref, *, tpb):
    first = (pl.program_id(0) % tpb) == 0
    acc = jnp.dot(a_ref[...], w_ref[...], preferred_element_type=F32)
    g = gt_ref[...]
    g0 = jnp.where(first, cgt_ref[...], g)
    o_ref[:CTX_ROWS, :] = r_ref[:CTX_ROWS, :] + g0 * acc[:CTX_ROWS]
    o_ref[CTX_ROWS:, :] = r_ref[CTX_ROWS:, :] + g * acc[CTX_ROWS:]


def _mm_res(a, w, wl, res, mods, layer, *, tm, tn, tpb):
    m, k = a.shape
    n = w.shape[2]
    return pl.pallas_call(
        functools.partial(_mm_res_kernel, tpb=tpb),
        grid=(m // tm, n // tn),
        in_specs=[pl.BlockSpec((tm, k), lambda i, j: (i, 0)),
                  pl.BlockSpec((None, k, tn), lambda i, j: (wl, 0, j)),
                  pl.BlockSpec((tm, tn), lambda i, j: (i, j)),
                  _mod_spec(layer, 2, lambda i, j: i // tpb, tn, lambda i, j: j),
                  _mod_spec(layer, 2, lambda i, j: 4, tn, lambda i, j: j)],
        out_specs=pl.BlockSpec((tm, tn), lambda i, j: (i, j)),
        out_shape=jax.ShapeDtypeStruct((m, n), F32),
        compiler_params=_cparams("parallel", "parallel"),
        name="mm_res",
    )(a, w, res, mods, mods)


def _mlp_kernel(x_ref, g_ref, sh_ref, sc_ref, gt_ref, csh_ref, csc_ref, cgt_ref, w1_ref, w2_ref,
                o_ref, hn_ref, *, tpb):
    i = pl.program_id(0)
    f = pl.program_id(1)
    first = (i % tpb) == 0

    @pl.when(f == 0)
    def _():
        _norm_mod_rows(x_ref, g_ref, sh_ref, sc_ref, csh_ref, csc_ref, first, hn_ref)
        o_ref[...] = jnp.zeros_like(o_ref)

    a = jnp.dot(hn_ref[...], w1_ref[...], preferred_element_type=F32)
    a = jnp.square(jnp.maximum(a, 0.0)).astype(BF16)
    o_ref[...] += jnp.dot(a, w2_ref[...], preferred_element_type=F32)

    @pl.when(f == pl.num_programs(1) - 1)
    def _():
        g = gt_ref[...]
        g0 = jnp.where(first, cgt_ref[...], g)
        o_ref[:CTX_ROWS, :] = x_ref[:CTX_ROWS, :] + g0 * o_ref[:CTX_ROWS, :]
        o_ref[CTX_ROWS:, :] = x_ref[CTX_ROWS:, :] + g * o_ref[CTX_ROWS:, :]


def _mlp(x, gains, mods, layer, w1, w2, *, tm, tf, tpb):
    m, d = x.shape
    dff = w1.shape[2]
    row = lambda i, f: i // tpb
    ctx = lambda i, f: 4
    return pl.pallas_call(
        functools.partial(_mlp_kernel, tpb=tpb),
        grid=(m // tm, dff // tf),
        in_specs=[pl.BlockSpec((tm, d), lambda i, f: (i, 0)),
                  _gain_spec(layer, d),
                  _mod_spec(layer, 3, row, d), _mod_spec(layer, 4, row, d), _mod_spec(layer, 5, row, d),
                  _mod_spec(layer, 3, ctx, d), _mod_spec(layer, 4, ctx, d), _mod_spec(layer, 5, ctx, d),
                  pl.BlockSpec((None, d, tf), lambda i, f: (layer, 0, f)),
                  pl.BlockSpec((None, tf, d), lambda i, f: (layer, f, 0))],
        out_specs=pl.BlockSpec((tm, d), lambda i, f: (i, 0)),
        out_shape=jax.ShapeDtypeStruct((m, d), F32),
        scratch_shapes=[pltpu.VMEM((tm, d), BF16)],
        compiler_params=_cparams("parallel", "arbitrary"),
        name="mlp",
    )(x, gains, mods, mods, mods, mods, mods, mods, w1, w2)


def _gelu_tanh(x):
    c = 0.7978845608028654
    hx = 0.5 * x
    return hx + hx * jnp.tanh(x * (c + (c * 0.044715) * (x * x)))


def _lru_kernel(gate_ref, u_ref, cw_ref, cb_ref, lam_ref, wa_ref, ba_ref, wx_ref, bx_ref, z_ref,
                upad, wcat, a_s, b_s, p_s, h_s, *, nseg):
    seg = LRU_BLOCK
    nh = LRU_BLOCK // LANES
    lat0 = CTX_ROWS + 16
    zeros8 = jnp.zeros((8, LANES), F32)
    for l in range(nh):
        lanes = slice(l * LANES, (l + 1) * LANES)
        upad[l, 0:8, :] = zeros8
        upad[l, 8:8 + CTX_ROWS, :] = u_ref[0:CTX_ROWS, lanes]
        upad[l, 8 + CTX_ROWS:lat0, :] = zeros8
        upad[l, lat0:lat0 + (nseg - 1) * seg, :] = u_ref[CTX_ROWS:, lanes]
        upad[l, lat0 + (nseg - 1) * seg:lat0 + (nseg - 1) * seg + 8, :] = zeros8
    for d in range(2):
        wcat[:, (2 * d) * seg:(2 * d + 1) * seg] = (0.5 * wa_ref[d]).astype(BF16)
        wcat[:, (2 * d + 1) * seg:(2 * d + 2) * seg] = (0.5 * wx_ref[d]).astype(BF16)

    nlam = -lam_ref[...]
    hcoef = (-0.5 * LRU_C) * (jnp.maximum(nlam, 0.0) + jnp.log1p(jnp.exp(-jnp.abs(nlam))))
    cw = cw_ref[...]
    cb = cb_ref[...]
    hba = 0.5 * ba_ref[...]
    hbx = 0.5 * bx_ref[...]

    def coeff_body(s, carry):
        base = s * seg + jnp.where(s == 0, 8, 16)
        ucs = []
        for l in range(nh):
            lanes = slice(l * LANES, (l + 1) * LANES)
            acc = cb[:, lanes]
            for k in range(CONV_WIDTH):
                acc = acc + cw[k:k + 1, lanes] * upad[l, pl.ds(base + (k - CONV_WIDTH // 2), seg), :]
            ucs.append(acc)
        uc = jnp.concatenate(ucs, axis=-1)
        gts = jnp.dot(uc.astype(BF16), wcat[...], preferred_element_type=F32)
        row0 = pl.multiple_of(s * SEG_PITCH, 8)
        for l in range(nh):
            lanes = slice(l * LANES, (l + 1) * LANES)
            huc = 0.5 * ucs[l]
            for d in range(2):
                ga = gts[:, 2 * d * seg + l * LANES:2 * d * seg + (l + 1) * LANES]
                gx = gts[:, (2 * d + 1) * seg + l * LANES:(2 * d + 1) * seg + (l + 1) * LANES]
                hc = hcoef[d:d + 1, lanes]
                log_a = jnp.tanh(ga + hba[d, :, lanes]) * hc + hc
                a = jnp.exp(log_a)
                gated_u = (jnp.tanh(gx + hbx[d, :, lanes]) + 1.0) * huc
                bb = jnp.sqrt(jnp.tanh(log_a) * (-1.0 - a * a)) * gated_u
                a_s[d, l, pl.ds(row0, seg), :] = a
                b_s[d, l, pl.ds(row0, seg), :] = bb
        return carry

    lax.fori_loop(0, nseg, coeff_body, 0)

    def ld(ref, d, l, j):
        v8 = ref[d, l, pl.ds(j, 8, stride=SEG_PITCH), :]
        v1 = ref[d, l, pl.ds(j + 8 * SEG_PITCH, 1), :]
        return v8, v1

    def st(ref, d, l, j, v8, v1):
        ref[d, l, pl.ds(j, 8, stride=SEG_PITCH), :] = v8
        ref[d, l, pl.ds(j + 8 * SEG_PITCH, 1), :] = v1

    def scan_body(t, carry):
        out = []
        for d in range(2):
            j = t if d == 0 else seg - 1 - t
            for l in range(nh):
                h8, h1, p8, p1 = carry[d * nh + l]
                a8, a1 = ld(a_s, d, l, j)
                b8, b1 = ld(b_s, d, l, j)
                h8 = a8 * h8 + b8
                h1 = a1 * h1 + b1
                p8 = a8 * p8
                p1 = a1 * p1
                st(p_s, d, l, j, p8, p1)
                st(h_s, d, l, j, h8, h1)
                out.append((h8, h1, p8, p1))
        return tuple(out)

    init = tuple((jnp.zeros((8, LANES), F32), jnp.zeros((1, LANES), F32),
                  jnp.ones((8, LANES), F32), jnp.ones((1, LANES), F32)) for _ in range(2 * nh))
    fin = lax.fori_loop(0, seg, scan_body, init, unroll=SCAN_UNROLL)

    def seg_row(v8, v1, s):
        return v1 if s == 8 else v8[s:s + 1, :]

    carries = []
    for d in range(2):
        per_l = []
        for l in range(nh):
            h8, h1, p8, p1 = fin[d * nh + l]
            order = list(range(nseg)) if d == 0 else [0] + list(range(nseg - 1, 0, -1))
            c = jnp.zeros((1, LANES), F32)
            cs = {}
            for s in order:
                cs[s] = c
                c = seg_row(h8, h1, s) + seg_row(p8, p1, s) * c
            per_l.append(cs)
        carries.append(per_l)

    for s in range(nseg):
        rows = slice(s * seg, (s + 1) * seg)
        srows = slice(s * SEG_PITCH, s * SEG_PITCH + seg)
        for l in range(nh):
            lanes = slice(l * LANES, (l + 1) * LANES)
            y = (h_s[0, l, srows, :] + p_s[0, l, srows, :] * carries[0][l][s]
                 + h_s[1, l, srows, :] + p_s[1, l, srows, :] * carries[1][l][s])
            z_ref[rows, lanes] = (_gelu_tanh(gate_ref[rows, lanes]) * y).astype(z_ref.dtype)


def _lru_core(gu, conv_w, conv_b, lam, wa, ba, wx, bx, jl, *, batch, t):
    m = gu.shape[0]
    width = gu.shape[1] // 2
    nb = width // LRU_BLOCK
    nseg = t // LRU_BLOCK
    assert nseg == 9, "scan kernel keeps 8 + 1 time segments on sublanes"
    nh = LRU_BLOCK // LANES
    srows = nseg * SEG_PITCH
    nl = conv_w.shape[0]
    wspec = pl.BlockSpec((None, 2, None, LRU_BLOCK, LRU_BLOCK), lambda b, n: (jl, 0, n, 0, 0))
    bspec = pl.BlockSpec((None, 2, None, 1, LRU_BLOCK), lambda b, n: (jl, 0, n, 0, 0))
    return pl.pallas_call(
        functools.partial(_lru_kernel, nseg=nseg),
        grid=(batch, nb),
        in_specs=[pl.BlockSpec((t, LRU_BLOCK), lambda b, n: (b, n)),
                  pl.BlockSpec((t, LRU_BLOCK), lambda b, n: (b, nb + n)),
                  pl.BlockSpec((None, CONV_WIDTH, LRU_BLOCK), lambda b, n: (jl, 0, n)),
                  pl.BlockSpec((None, 1, LRU_BLOCK), lambda b, n: (jl, 0, n)),
                  pl.BlockSpec((None, 2, LRU_BLOCK), lambda b, n: (jl, 0, n)),
                  wspec, bspec, wspec, bspec],
        out_specs=pl.BlockSpec((t, LRU_BLOCK), lambda b, n: (b, n)),
        out_shape=jax.ShapeDtypeStruct((m, width), BF16),
        scratch_shapes=[pltpu.VMEM((nh, t + 24, LANES), F32),
                        pltpu.VMEM((LRU_BLOCK, 4 * LRU_BLOCK), BF16),
                        pltpu.VMEM((2, nh, srows, LANES), F32),
                        pltpu.VMEM((2, nh, srows, LANES), F32),
                        pltpu.VMEM((2, nh, srows, LANES), F32),
                        pltpu.VMEM((2, nh, srows, LANES), F32)],
        compiler_params=_cparams("parallel", "parallel"),
        name="lru_core",
    )(gu, gu, conv_w, conv_b.reshape(nl, 1, width), lam, wa, ba.reshape(nl, 2, nb, 1, LRU_BLOCK),
      wx, bx.reshape(nl, 2, nb, 1, LRU_BLOCK))


_NT = (((1,), (1,)), ((), ()))
_EXP2_SCALE = (HEAD_DIM ** -0.5) * math.log2(math.e)


def _softmax_rows(parts):
    m = parts[0].max(axis=-1, keepdims=True)
    for p in parts[1:]:
        m = jnp.maximum(m, p.max(axis=-1, keepdims=True))
    es = [jnp.exp2((p - m) * _EXP2_SCALE) for p in parts]
    tot = es[0].sum(axis=-1, keepdims=True)
    for e in es[1:]:
        tot = tot + e.sum(axis=-1, keepdims=True)
    inv = 1.0 / tot
    return [(e * inv).astype(BF16) for e in es]


def _na_kernel(q_ref, k_ref, v_ref, bias_ref, o_ref, *, n_qblocks):
    qb = QROWS * GRID_W
    kc = k_ref[0:CTX_ROWS, :]
    vc = v_ref[0:CTX_ROWS, :]

    s_c = lax.dot_general(q_ref[0:CTX_ROWS, :], kc, _NT, preferred_element_type=F32)
    (p_c,) = _softmax_rows([s_c])
    o_ref[0:CTX_ROWS, :] = jnp.dot(p_c, vc, preferred_element_type=F32).astype(o_ref.dtype)

    n_rows = n_qblocks * QROWS

    def body(jj, carry):
        r0 = pl.multiple_of(CTX_ROWS + jj * qb, qb)
        q = q_ref[pl.ds(r0, qb), :]
        ws = jnp.clip(jj * QROWS - WIN_ROWS // 2, 0, n_rows - KROWS)
        start = pl.multiple_of(CTX_ROWS + ws * GRID_W, CTX_ROWS)
        var = jnp.where(jj == 0, 0, jnp.where(jj == n_qblocks - 1, 2, 1))
        kw = k_ref[pl.ds(start, KROWS * GRID_W), :]
        vw = v_ref[pl.ds(start, KROWS * GRID_W), :]
        s_loc = lax.dot_general(q, kw, _NT, preferred_element_type=F32) + bias_ref[var]
        s_ctx = lax.dot_general(q, kc, _NT, preferred_element_type=F32)
        p_loc, p_ctx = _softmax_rows([s_loc, s_ctx])
        o = jnp.dot(p_loc, vw, preferred_element_type=F32) + jnp.dot(p_ctx, vc, preferred_element_type=F32)
        o_ref[pl.ds(r0, qb), :] = o.astype(o_ref.dtype)
        return carry

    lax.fori_loop(0, n_qblocks, body, 0, unroll=2)


def _na_bias_tables(rpb, n_rows):
    nl, n_heads = rpb.shape[:2]
    qc = np.arange(GRID_W)[:, None]
    kc = np.arange(GRID_W)[None, :]
    cs = np.clip(qc - WIN_COLS // 2, 0, GRID_W - WIN_COLS)
    col_ok = (kc >= cs) & (kc < cs + WIN_COLS)
    col_off = np.where(col_ok, kc - qc + WIN_COLS - 1, 0)
    onehot = (np.arange(2 * WIN_COLS - 1)[:, None, None] == col_off[None]) & col_ok[None]
    onehot = jnp.asarray(onehot.astype(np.float32))
    t2 = jnp.einsum('lhrm,mqk->lhrqk', rpb.astype(F32) * (HEAD_DIM ** 0.5), onehot,
                    precision=lax.Precision.HIGHEST)
    t2 = jnp.where(jnp.asarray(col_ok)[None, None, None], t2, NEG)
    n_dr = 2 * WIN_ROWS - 1
    t2 = jnp.concatenate([t2, jnp.full((nl, n_heads, 1, GRID_W, GRID_W), NEG, F32)], axis=2)
    idx = np.full((3, QROWS, KROWS), n_dr, np.int32)
    n_qblocks = n_rows // QROWS
    for v, jj in enumerate((0, 1, n_qblocks - 1)):
        r0 = jj * QROWS
        ws = int(np.clip(r0 - WIN_ROWS // 2, 0, n_rows - KROWS))
        for a in range(QROWS):
            qr = r0 + a
            rs = int(np.clip(qr - WIN_ROWS // 2, 0, n_rows - WIN_ROWS))
            for kr in range(KROWS):
                kra = ws + kr
                if rs <= kra < rs + WIN_ROWS:
                    idx[v, a, kr] = kra - qr + WIN_ROWS - 1
    full = t2[:, :, idx]
    full = jnp.transpose(full, (0, 1, 2, 3, 5, 4, 6))
    return full.reshape(nl, n_heads, 3, QROWS * GRID_W, KROWS * GRID_W)


def _na_attention(qkv, bias, jl, *, batch, t):
    m = qkv.shape[0]
    d = qkv.shape[1] // 3
    n_heads = d // HEAD_DIM
    qb = QROWS * GRID_W
    n_qblocks = (t - CTX_ROWS) // qb
    return pl.pallas_call(
        functools.partial(_na_kernel, n_qblocks=n_qblocks),
        grid=(n_heads, batch),
        in_specs=[pl.BlockSpec((t, HEAD_DIM), lambda h, b: (b, h)),
                  pl.BlockSpec((t, HEAD_DIM), lambda h, b: (b, n_heads + h)),
                  pl.BlockSpec((t, HEAD_DIM), lambda h, b: (b, 2 * n_heads + h)),
                  pl.BlockSpec((None, None, 3, qb, KROWS * GRID_W), lambda h, b: (jl, h, 0, 0, 0))],
        out_specs=pl.BlockSpec((t, HEAD_DIM), lambda h, b: (b, h)),
        out_shape=jax.ShapeDtypeStruct((m, d), BF16),
        compiler_params=_cparams("parallel", "parallel"),
        name="na_attn",
    )(qkv, qkv, qkv, bias)


def _final_norm_kernel(x_ref, g_ref, o_ref):
    g = g_ref[...]

    def body(r, carry):
        rows = pl.ds(pl.multiple_of(r * NORM_CHUNK, NORM_CHUNK), NORM_CHUNK)
        xs = x_ref[rows, :]
        ms = jnp.mean(xs * xs, axis=-1, keepdims=True)
        o_ref[rows, :] = xs * lax.rsqrt(ms + EPS) * g
        return carry

    lax.fori_loop(0, x_ref.shape[0] // NORM_CHUNK, body, 0)


def _final_norm(x, g, *, batch, t, seq):
    d = x.shape[1]
    nblk = seq // CTX_ROWS
    return pl.pallas_call(
        _final_norm_kernel,
        grid=(batch, nblk),
        in_specs=[pl.BlockSpec((None, CTX_ROWS, d), lambda b, j: (b, 1 + j, 0)),
                  pl.BlockSpec((1, d), lambda b, j: (0, 0))],
        out_specs=pl.BlockSpec((None, CTX_ROWS, d), lambda b, j: (b, j, 0)),
        out_shape=jax.ShapeDtypeStruct((batch, seq, d), F32),
        compiler_params=_cparams("parallel", "parallel"),
        name="final_norm",
    )(x.reshape(batch, t, d), g)


def kernel(x, c, ctx, c_ctx, ada_w, ada_b, norm1_g, norm2_g, mlp_w1, mlp_w2, lru_w_in, lru_conv_w, lru_conv_b, lru_lambda, lru_wa, lru_ba, lru_wx, lru_bx, lru_w_out, na_w_qkv, na_rpb, na_w_o, final_g):
    batch, seq, d = x.shape
    ctx_len = ctx.shape[1]
    depth = ada_w.shape[0]
    assert ctx_len == CTX_ROWS and batch <= 4 and seq % (QROWS * GRID_W) == 0
    t = ctx_len + seq
    tpb = 3
    tm = t // tpb
    assert tm % CTX_ROWS == 0
    tf = min(1024, mlp_w1.shape[2])

    def col_tile(n, target):
        return max(c for c in range(LANES, target + 1, LANES) if n % c == 0)

    tn_in = col_tile(lru_w_in.shape[2], 1536)
    tn_qkv = col_tile(na_w_qkv.shape[2], 1536)
    tn_res = col_tile(d, 1024)

    xs = jnp.concatenate([ctx, x], axis=1).reshape(batch * t, d)
    c8 = jnp.zeros((8, d), F32).at[:batch].set(c).at[4].set(c_ctx)
    mods = _ada_mods(c8, ada_w, ada_b).reshape(depth, 8, N_MOD, 1, d)
    g1 = norm1_g.reshape(depth, 1, d)
    g2 = norm2_g.reshape(depth, 1, d)
    w1, w2 = mlp_w1.astype(BF16), mlp_w2.astype(BF16)
    w_in, w_out = lru_w_in.astype(BF16), lru_w_out.astype(BF16)
    w_qkv, w_o = na_w_qkv.astype(BF16), na_w_o.astype(BF16)
    bias = _na_bias_tables(na_rpb, seq // GRID_W)

    for i in range(depth):
        j = i // 2
        if i % 2 == 0:
            gu = _mm_norm(xs, g1, mods, i, w_in, j, F32, tm=tm, tn=tn_in, tpb=tpb)
            z = _lru_core(gu, lru_conv_w, lru_conv_b, lru_lambda, lru_wa, lru_ba, lru_wx, lru_bx, j,
                          batch=batch, t=t)
            xs = _mm_res(z, w_out, j, xs, mods, i, tm=tm, tn=tn_res, tpb=tpb)
        else:
            qkv = _mm_norm(xs, g1, mods, i, w_qkv, j, BF16, tm=tm, tn=tn_qkv, tpb=tpb)
            o = _na_attention(qkv, bias, j, batch=batch, t=t)
            xs = _mm_res(o, w_o, j, xs, mods, i, tm=tm, tn=tn_res, tpb=tpb)
        xs = _mlp(xs, g2, mods, i, w1, w2, tm=tm, tf=tf, tpb=tpb)

    return _final_norm(xs, final_g.reshape(1, d), batch=batch, t=t, seq=seq)
```

```python
import functools
import math

import numpy as np
import jax
import jax.numpy as jnp
from jax import lax
from jax.experimental import pallas as pl
from jax.experimental.pallas import tpu as pltpu

GRID_W = 64
WIN_ROWS = 8
WIN_COLS = 16
HEAD_DIM = 128
LRU_BLOCK = 256
CONV_WIDTH = 4
LRU_C = 8.0
N_MOD = 6
EPS = 1e-6

LANES = 128
CTX_ROWS = 256
NORM_CHUNK = 32
QROWS = 4
KROWS = QROWS + WIN_ROWS
NA_CHUNK = 32
SEG_PITCH = LRU_BLOCK + 8
SCAN_UNROLL = 4
NEG = -1e30
VMEM_LIMIT = 56 * 1024 * 1024

F32 = jnp.float32
BF16 = jnp.bfloat16


def _cparams(*sem):
    return pltpu.CompilerParams(dimension_semantics=sem, vmem_limit_bytes=VMEM_LIMIT)


def _mod_spec(layer, k, row_fn, width, col_fn=None):
    if col_fn is None:
        return pl.BlockSpec((None, None, None, 1, width),
                            lambda *g: (layer, row_fn(*g), k, 0, 0))
    return pl.BlockSpec((None, None, None, 1, width),
                        lambda *g: (layer, row_fn(*g), k, 0, col_fn(*g)))


def _gain_spec(layer, d):
    return pl.BlockSpec((None, 1, d), lambda *g: (layer, 0, 0))


def _ada_kernel(c_ref, w_ref, b_ref, o_ref):
    c = c_ref[...]
    s = (c * jax.nn.sigmoid(c)).astype(BF16)
    o_ref[...] = jnp.dot(s, w_ref[...].astype(BF16), preferred_element_type=F32) + b_ref[...]


def _ada_mods(c8, ada_w, ada_b, tn=1024):
    depth, d, n = ada_w.shape
    tn = math.gcd(tn, n)
    return pl.pallas_call(
        _ada_kernel,
        grid=(depth, n // tn),
        in_specs=[pl.BlockSpec((8, d), lambda l, j: (0, 0)),
                  pl.BlockSpec((None, d, tn), lambda l, j: (l, 0, j)),
                  pl.BlockSpec((None, 1, tn), lambda l, j: (l, 0, j))],
        out_specs=pl.BlockSpec((None, 8, tn), lambda l, j: (l, 0, j)),
        out_shape=jax.ShapeDtypeStruct((depth, 8, n), F32),
        compiler_params=_cparams("parallel", "parallel"),
        name="ada_mods",
    )(c8, ada_w, ada_b.reshape(depth, 1, n))


def _norm_mod_rows(x_ref, g_ref, sh_ref, sc_ref, csh_ref, csc_ref, first, hn_ref, rinv_ref):
    tm, d = x_ref.shape

    def rms_body(r, carry):
        rows = pl.ds(pl.multiple_of(r * NORM_CHUNK, NORM_CHUNK), NORM_CHUNK)
        xs = x_ref[rows, :]
        ms = jnp.mean(xs * xs, axis=-1, keepdims=True)
        rinv_ref[rows, :] = jnp.broadcast_to(lax.rsqrt(ms + EPS), (NORM_CHUNK, LANES))
        return carry

    lax.fori_loop(0, tm // NORM_CHUNK, rms_body, 0, unroll=4)

    g = g_ref[...]
    sh = sh_ref[...]
    gain = g * (1.0 + sc_ref[...])
    sh0 = jnp.where(first, csh_ref[...], sh)
    gain0 = jnp.where(first, g * (1.0 + csc_ref[...]), gain)

    def scale_rows(lo, hi, gain_row, shift_row):
        def body(r, carry):
            rows = pl.ds(pl.multiple_of(r * NORM_CHUNK, NORM_CHUNK), NORM_CHUNK)
            rinv = rinv_ref[rows, :]
            for c in range(d // LANES):
                lanes = slice(c * LANES, (c + 1) * LANES)
                hn_ref[rows, lanes] = (x_ref[rows, lanes] * rinv * gain_row[:, lanes]
                                       + shift_row[:, lanes]).astype(hn_ref.dtype)
            return carry

        lax.fori_loop(lo, hi, body, 0, unroll=2)

    scale_rows(0, CTX_ROWS // NORM_CHUNK, gain0, sh0)
    scale_rows(CTX_ROWS // NORM_CHUNK, tm // NORM_CHUNK, gain, sh)


def _mm_norm_kernel(x_ref, g_ref, sh_ref, sc_ref, csh_ref, csc_ref, w_ref, o_ref, hn_ref, rinv_ref,
                    *, tpb):
    i = pl.program_id(0)

    @pl.when(pl.program_id(1) == 0)
    def _():
        _norm_mod_rows(x_ref, g_ref, sh_ref, sc_ref, csh_ref, csc_ref, (i % tpb) == 0, hn_ref, rinv_ref)

    o_ref[...] = jnp.dot(hn_ref[...], w_ref[...], preferred_element_type=F32).astype(o_ref.dtype)


def _mm_norm(x, gains, mods, layer, w, wl, out_dtype, *, tm, tn, tpb):
    m, d = x.shape
    n = w.shape[2]
    row = lambda i, j: i // tpb
    ctx = lambda i, j: 4
    return pl.pallas_call(
        functools.partial(_mm_norm_kernel, tpb=tpb),
        grid=(m // tm, n // tn),
        in_specs=[pl.BlockSpec((tm, d), lambda i, j: (i, 0)),
                  _gain_spec(layer, d),
                  _mod_spec(layer, 0, row, d), _mod_spec(layer, 1, row, d),
                  _mod_spec(layer, 0, ctx, d), _mod_spec(layer, 1, ctx, d),
                  pl.BlockSpec((None, d, tn), lambda i, j: (wl, 0, j))],
        out_specs=pl.BlockSpec((tm, tn), lambda i, j: (i, j)),
        out_shape=jax.ShapeDtypeStruct((m, n), out_dtype),
        scratch_shapes=[pltpu.VMEM((tm, d), BF16), pltpu.VMEM((tm, LANES), F32)],
        compiler_params=_cparams("parallel", "arbitrary"),
        name="mm_norm",
    )(x, gains, mods, mods, mods, mods, w)


def _mm_res_kernel(a_ref, w_ref, r_ref, gt_ref, cgt_ref, o_ref, *, tpb):
    first = (pl.program_id(0) % tpb) == 0
    acc = jnp.dot(a_ref[...], w_ref[...], preferred_element_type=F32)
    g = gt_ref[...]
    g0 = jnp.where(first, cgt_ref[...], g)
    o_ref[:CTX_ROWS, :] = r_ref[:CTX_ROWS, :] + g0 * acc[:CTX_ROWS]
    o_ref[CTX_ROWS:, :] = r_ref[CTX_ROWS:, :] + g * acc[CTX_ROWS:]


def _mm_res(a, w, wl, res, mods, layer, *, tm, tn, tpb):
    m, k = a.shape
    n = w.shape[2]
    return pl.pallas_call(
        functools.partial(_mm_res_kernel, tpb=tpb),
        grid=(m // tm, n // tn),
        in_specs=[pl.BlockSpec((tm, k), lambda i, j: (i, 0)),
                  pl.BlockSpec((None, k, tn), lambda i, j: (wl, 0, j)),
                  pl.BlockSpec((tm, tn), lambda i, j: (i, j)),
                  _mod_spec(layer, 2, lambda i, j: i // tpb, tn, lambda i, j: j),
                  _mod_spec(layer, 2, lambda i, j: 4, tn, lambda i, j: j)],
        out_specs=pl.BlockSpec((tm, tn), lambda i, j: (i, j)),
        out_shape=jax.ShapeDtypeStruct((m, n), F32),
        compiler_params=_cparams("parallel", "parallel"),
        name="mm_res",
    )(a, w, res, mods, mods)


def _mlp_kernel(x_ref, g_ref, sh_ref, sc_ref, gt_ref, csh_ref, csc_ref, cgt_ref, w1_ref, w2_ref,
                o_ref, hn_ref, rinv_ref, *, tpb):
    i = pl.program_id(0)
    f = pl.program_id(1)
    first = (i % tpb) == 0

    @pl.when(f == 0)
    def _():
        _norm_mod_rows(x_ref, g_ref, sh_ref, sc_ref, csh_ref, csc_ref, first, hn_ref, rinv_ref)
        o_ref[...] = jnp.zeros_like(o_ref)

    a = jnp.dot(hn_ref[...], w1_ref[...], preferred_element_type=F32)
    a = jnp.square(jnp.maximum(a, 0.0)).astype(BF16)
    o_ref[...] += jnp.dot(a, w2_ref[...], preferred_element_type=F32)

    @pl.when(f == pl.num_programs(1) - 1)
    def _():
        g = gt_ref[...]
        g0 = jnp.where(first, cgt_ref[...], g)
        o_ref[:CTX_ROWS, :] = x_ref[:CTX_ROWS, :] + g0 * o_ref[:CTX_ROWS, :]
        o_ref[CTX_ROWS:, :] = x_ref[CTX_ROWS:, :] + g * o_ref[CTX_ROWS:, :]


def _mlp(x, gains, mods, layer, w1, w2, *, tm, tf, tpb):
    m, d = x.shape
    dff = w1.shape[2]
    row = lambda i, f: i // tpb
    ctx = lambda i, f: 4
    return pl.pallas_call(
        functools.partial(_mlp_kernel, tpb=tpb),
        grid=(m // tm, dff // tf),
        in_specs=[pl.BlockSpec((tm, d), lambda i, f: (i, 0)),
                  _gain_spec(layer, d),
                  _mod_spec(layer, 3, row, d), _mod_spec(layer, 4, row, d), _mod_spec(layer, 5, row, d),
                  _mod_spec(layer, 3, ctx, d), _mod_spec(layer, 4, ctx, d), _mod_spec(layer, 5, ctx, d),
                  pl.BlockSpec((None, d, tf), lambda i, f: (layer, 0, f)),
                  pl.BlockSpec((None, tf, d), lambda i, f: (layer, f, 0))],
        out_specs=pl.BlockSpec((tm, d), lambda i, f: (i, 0)),
        out_shape=jax.ShapeDtypeStruct((m, d), F32),
        scratch_shapes=[pltpu.VMEM((tm, d), BF16), pltpu.VMEM((tm, LANES), F32)],
        compiler_params=_cparams("parallel", "arbitrary"),
        name="mlp",
    )(x, gains, mods, mods, mods, mods, mods, mods, w1, w2)


def _gelu_tanh(x):
    c = 0.7978845608028654
    hx = 0.5 * x
    return hx + hx * jnp.tanh(x * (c + (c * 0.044715) * (x * x)))


def _lru_kernel(gate_ref, u_ref, cw_ref, cb_ref, lam_ref, wa_ref, ba_ref, wx_ref, bx_ref, z_ref,
                upad, wcat, a_s, b_s, p_s, h_s, *, nseg):
    seg = LRU_BLOCK
    nh = LRU_BLOCK // LANES
    lat0 = CTX_ROWS + 16
    zeros8 = jnp.zeros((8, LANES), F32)
    for l in range(nh):
        lanes = slice(l * LANES, (l + 1) * LANES)
        upad[l, 0:8, :] = zeros8
        upad[l, 8:8 + CTX_ROWS, :] = u_ref[0:CTX_ROWS, lanes]
        upad[l, 8 + CTX_ROWS:lat0, :] = zeros8
        upad[l, lat0:lat0 + (nseg - 1) * seg, :] = u_ref[CTX_ROWS:, lanes]
        upad[l, lat0 + (nseg - 1) * seg:lat0 + (nseg - 1) * seg + 8, :] = zeros8
    for d in range(2):
        wcat[:, (2 * d) * seg:(2 * d + 1) * seg] = (0.5 * wa_ref[d]).astype(BF16)
        wcat[:, (2 * d + 1) * seg:(2 * d + 2) * seg] = (0.5 * wx_ref[d]).astype(BF16)

    nlam = -lam_ref[...]
    hcoef = (-0.5 * LRU_C) * (jnp.maximum(nlam, 0.0) + jnp.log1p(jnp.exp(-jnp.abs(nlam))))
    cw = cw_ref[...]
    cb = cb_ref[...]
    hba = 0.5 * ba_ref[...]
    hbx = 0.5 * bx_ref[...]

    def coeff_body(s, carry):
        base = s * seg + jnp.where(s == 0, 8, 16)
        ucs = []
        for l in range(nh):
            lanes = slice(l * LANES, (l + 1) * LANES)
            acc = cb[:, lanes]
            for k in range(CONV_WIDTH):
                acc = acc + cw[k:k + 1, lanes] * upad[l, pl.ds(base + (k - CONV_WIDTH // 2), seg), :]
            ucs.append(acc)
        uc = jnp.concatenate(ucs, axis=-1)
        gts = jnp.dot(uc.astype(BF16), wcat[...], preferred_element_type=F32)
        row0 = pl.multiple_of(s * SEG_PITCH, 8)
        for l in range(nh):
            lanes = slice(l * LANES, (l + 1) * LANES)
            huc = 0.5 * ucs[l]
            for d in range(2):
                ga = gts[:, 2 * d * seg + l * LANES:2 * d * seg + (l + 1) * LANES]
                gx = gts[:, (2 * d + 1) * seg + l * LANES:(2 * d + 1) * seg + (l + 1) * LANES]
                hc = hcoef[d:d + 1, lanes]
                log_a = jnp.tanh(ga + hba[d, :, lanes]) * hc + hc
                a = jnp.exp(log_a)
                gated_u = (jnp.tanh(gx + hbx[d, :, lanes]) + 1.0) * huc
                bb = jnp.sqrt(jnp.tanh(log_a) * (-1.0 - a * a)) * gated_u
                a_s[d, l, pl.ds(row0, seg), :] = a
                b_s[d, l, pl.ds(row0, seg), :] = bb
        return carry

    lax.fori_loop(0, nseg, coeff_body, 0)

    def ld(ref, d, l, j):
        v8 = ref[d, l, pl.ds(j, 8, stride=SEG_PITCH), :]
        v1 = ref[d, l, pl.ds(j + 8 * SEG_PITCH, 1), :]
        return v8, v1

    def st(ref, d, l, j, v8, v1):
        ref[d, l, pl.ds(j, 8, stride=SEG_PITCH), :] = v8
        ref[d, l, pl.ds(j + 8 * SEG_PITCH, 1), :] = v1

    def scan_body(t, carry):
        out = []
        for d in range(2):
            j = t if d == 0 else seg - 1 - t
            for l in range(nh):
                h8, h1, p8, p1 = carry[d * nh + l]
                a8, a1 = ld(a_s, d, l, j)
                b8, b1 = ld(b_s, d, l, j)
                h8 = a8 * h8 + b8
                h1 = a1 * h1 + b1
                p8 = a8 * p8
                p1 = a1 * p1
                st(p_s, d, l, j, p8, p1)
                st(h_s, d, l, j, h8, h1)
                out.append((h8, h1, p8, p1))
        return tuple(out)

    init = tuple((jnp.zeros((8, LANES), F32), jnp.zeros((1, LANES), F32),
                  jnp.ones((8, LANES), F32), jnp.ones((1, LANES), F32)) for _ in range(2 * nh))
    fin = lax.fori_loop(0, seg, scan_body, init, unroll=SCAN_UNROLL)

    def seg_row(v8, v1, s):
        return v1 if s == 8 else v8[s:s + 1, :]

    carries = []
    for d in range(2):
        per_l = []
        for l in range(nh):
            h8, h1, p8, p1 = fin[d * nh + l]
            order = list(range(nseg)) if d == 0 else [0] + list(range(nseg - 1, 0, -1))
            c = jnp.zeros((1, LANES), F32)
            cs = {}
            for s in order:
                cs[s] = c
                c = seg_row(h8, h1, s) + seg_row(p8, p1, s) * c
            per_l.append(cs)
        carries.append(per_l)

    for s in range(nseg):
        rows = slice(s * seg, (s + 1) * seg)
        srows = slice(s * SEG_PITCH, s * SEG_PITCH + seg)
        for l in range(nh):
            lanes = slice(l * LANES, (l + 1) * LANES)
            y = (h_s[0, l, srows, :] + p_s[0, l, srows, :] * carries[0][l][s]
                 + h_s[1, l, srows, :] + p_s[1, l, srows, :] * carries[1][l][s])
            z_ref[rows, lanes] = (_gelu_tanh(gate_ref[rows, lanes]) * y).astype(z_ref.dtype)


def _lru_core(gu, conv_w, conv_b, lam, wa, ba, wx, bx, jl, *, batch, t):
    m = gu.shape[0]
    width = gu.shape[1] // 2
    nb = width // LRU_BLOCK
    nseg = t // LRU_BLOCK
    assert nseg == 9, "scan kernel keeps 8 + 1 time segments on sublanes"
    nh = LRU_BLOCK // LANES
    srows = nseg * SEG_PITCH
    nl = conv_w.shape[0]
    wspec = pl.BlockSpec((None, 2, None, LRU_BLOCK, LRU_BLOCK), lambda b, n: (jl, 0, n, 0, 0))
    bspec = pl.BlockSpec((None, 2, None, 1, LRU_BLOCK), lambda b, n: (jl, 0, n, 0, 0))
    return pl.pallas_call(
        functools.partial(_lru_kernel, nseg=nseg),
        grid=(batch, nb),
        in_specs=[pl.BlockSpec((t, LRU_BLOCK), lambda b, n: (b, n)),
                  pl.BlockSpec((t, LRU_BLOCK), lambda b, n: (b, nb + n)),
                  pl.BlockSpec((None, CONV_WIDTH, LRU_BLOCK), lambda b, n: (jl, 0, n)),
                  pl.BlockSpec((None, 1, LRU_BLOCK), lambda b, n: (jl, 0, n)),
                  pl.BlockSpec((None, 2, LRU_BLOCK), lambda b, n: (jl, 0, n)),
                  wspec, bspec, wspec, bspec],
        out_specs=pl.BlockSpec((t, LRU_BLOCK), lambda b, n: (b, n)),
        out_shape=jax.ShapeDtypeStruct((m, width), BF16),
        scratch_shapes=[pltpu.VMEM((nh, t + 24, LANES), F32),
                        pltpu.VMEM((LRU_BLOCK, 4 * LRU_BLOCK), BF16),
                        pltpu.VMEM((2, nh, srows, LANES), F32),
                        pltpu.VMEM((2, nh, srows, LANES), F32),
                        pltpu.VMEM((2, nh, srows, LANES), F32),
                        pltpu.VMEM((2, nh, srows, LANES), F32)],
        compiler_params=_cparams("parallel", "parallel"),
        name="lru_core",
    )(gu, gu, conv_w, conv_b.reshape(nl, 1, width), lam, wa, ba.reshape(nl, 2, nb, 1, LRU_BLOCK),
      wx, bx.reshape(nl, 2, nb, 1, LRU_BLOCK))


_NT = (((1,), (1,)), ((), ()))
_EXP2_SCALE = (HEAD_DIM ** -0.5) * math.log2(math.e)


def _softmax_rows(parts):
    m = parts[0].max(axis=-1, keepdims=True)
    for p in parts[1:]:
        m = jnp.maximum(m, p.max(axis=-1, keepdims=True))
    es = [jnp.exp2((p - m) * _EXP2_SCALE) for p in parts]
    tot = es[0].sum(axis=-1, keepdims=True)
    for e in es[1:]:
        tot = tot + e.sum(axis=-1, keepdims=True)
    inv = 1.0 / tot
    return [(e * inv).astype(BF16) for e in es]


def _na_kernel(q_ref, k_ref, v_ref, pairs_ref, o_ref, s_a, s_b, p_a, p_b, *, n_qblocks, tab):
    qb = QROWS * GRID_W
    kc = k_ref[0:CTX_ROWS, :]
    vc = v_ref[0:CTX_ROWS, :]

    s_c = lax.dot_general(q_ref[0:CTX_ROWS, :], kc, _NT, preferred_element_type=F32)
    (p_c,) = _softmax_rows([s_c])
    o_ref[0:CTX_ROWS, :] = jnp.dot(p_c, vc, preferred_element_type=F32).astype(o_ref.dtype)

    n_rows = n_qblocks * QROWS
    nloc = KROWS * GRID_W

    def q_rows(jj):
        return pl.ds(pl.multiple_of(CTX_ROWS + jj * qb, qb), qb)

    def window(jj):
        ws = jnp.clip(jj * QROWS - WIN_ROWS // 2, 0, n_rows - KROWS)
        return pl.ds(pl.multiple_of(CTX_ROWS + ws * GRID_W, CTX_ROWS), nloc)

    def logits(jj, s_ref):
        q = q_ref[q_rows(jj), :]
        s_ref[:, :nloc] = lax.dot_general(q, k_ref[window(jj), :], _NT, preferred_element_type=F32)
        s_ref[:, nloc:] = lax.dot_general(q, kc, _NT, preferred_element_type=F32)

    def probs(jj, s_ref, p_ref):
        var = jnp.where(jj == 0, 0, jnp.where(jj == n_qblocks - 1, 2, 1))
        for c in range(qb // NA_CHUNK):
            a, off = divmod(c * NA_CHUNK, GRID_W)
            rows = slice(c * NA_CHUNK, (c + 1) * NA_CHUNK)
            tiles = []
            for m in range(KROWS // 2):
                e0, e1, e2 = (tab[v][a][m] for v in range(3))
                e = e1 if e0 == e1 == e2 else jnp.where(var == 0, e0, jnp.where(var == 1, e1, e2))
                tiles.append(pairs_ref[e, off:off + NA_CHUNK, :])
            s_loc = s_ref[rows, :nloc] + jnp.concatenate(tiles, axis=-1)
            p_loc, p_ctx = _softmax_rows([s_loc, s_ref[rows, nloc:]])
            p_ref[rows, :nloc] = p_loc
            p_ref[rows, nloc:] = p_ctx

    def attend(jj, p_ref):
        o = (jnp.dot(p_ref[:, :nloc], v_ref[window(jj), :], preferred_element_type=F32)
             + jnp.dot(p_ref[:, nloc:], vc, preferred_element_type=F32))
        o_ref[q_rows(jj), :] = o.astype(o_ref.dtype)

    logits(0, s_a)

    def body(it, carry):
        j0 = 2 * it
        j1 = j0 + 1
        j2 = jnp.minimum(j0 + 2, n_qblocks - 1)
        logits(j1, s_b)
        probs(j0, s_a, p_a)
        attend(j0, p_a)
        probs(j1, s_b, p_b)
        logits(j2, s_a)
        attend(j1, p_b)
        return carry

    lax.fori_loop(0, n_qblocks // 2, body, 0)


def _na_window_plan(n_rows):
    n_dr = 2 * WIN_ROWS - 1
    idx = np.full((3, QROWS, KROWS), n_dr, np.int32)
    n_qblocks = n_rows // QROWS
    for v, jj in enumerate((0, 1, n_qblocks - 1)):
        r0 = jj * QROWS
        ws = int(np.clip(r0 - WIN_ROWS // 2, 0, n_rows - KROWS))
        for a in range(QROWS):
            qr = r0 + a
            rs = int(np.clip(qr - WIN_ROWS // 2, 0, n_rows - WIN_ROWS))
            for kr in range(KROWS):
                kra = ws + kr
                if rs <= kra < rs + WIN_ROWS:
                    idx[v, a, kr] = kra - qr + WIN_ROWS - 1
    pairs = sorted({(int(idx[v, a, 2 * m]), int(idx[v, a, 2 * m + 1]))
                    for v in range(3) for a in range(QROWS) for m in range(KROWS // 2)})
    tab = tuple(tuple(tuple(pairs.index((int(idx[v, a, 2 * m]), int(idx[v, a, 2 * m + 1])))
                            for m in range(KROWS // 2)) for a in range(QROWS)) for v in range(3))
    return pairs, tab


def _na_bias_pairs(rpb, pairs):
    nl, n_heads = rpb.shape[:2]
    qc = np.arange(GRID_W)[:, None]
    kc = np.arange(GRID_W)[None, :]
    cs = np.clip(qc - WIN_COLS // 2, 0, GRID_W - WIN_COLS)
    col_ok = (kc >= cs) & (kc < cs + WIN_COLS)
    col_off = np.where(col_ok, kc - qc + WIN_COLS - 1, 0)
    onehot = (np.arange(2 * WIN_COLS - 1)[:, None, None] == col_off[None]) & col_ok[None]
    onehot = jnp.asarray(onehot.astype(np.float32))
    t2 = jnp.einsum('lhrm,mqk->lhrqk', rpb.astype(F32) * (HEAD_DIM ** 0.5), onehot,
                    precision=lax.Precision.HIGHEST)
    t2 = jnp.where(jnp.asarray(col_ok)[None, None, None], t2, NEG)
    masked = jnp.full((nl, n_heads, GRID_W, GRID_W), NEG, F32)
    rows = [t2[:, :, r] for r in range(2 * WIN_ROWS - 1)] + [masked]
    tiles = [jnp.concatenate([rows[left], rows[right]], axis=-1) for left, right in pairs]
    return jnp.stack(tiles, axis=2)


def _na_attention(qkv, bias_pairs, tab, jl, *, batch, t):
    m = qkv.shape[0]
    d = qkv.shape[1] // 3
    n_heads = d // HEAD_DIM
    qb = QROWS * GRID_W
    n_qblocks = (t - CTX_ROWS) // qb
    n_pairs = bias_pairs.shape[2]
    n_keys = KROWS * GRID_W + CTX_ROWS
    assert n_qblocks % 2 == 0
    return pl.pallas_call(
        functools.partial(_na_kernel, n_qblocks=n_qblocks, tab=tab),
        grid=(n_heads, batch),
        in_specs=[pl.BlockSpec((t, HEAD_DIM), lambda h, b: (b, h)),
                  pl.BlockSpec((t, HEAD_DIM), lambda h, b: (b, n_heads + h)),
                  pl.BlockSpec((t, HEAD_DIM), lambda h, b: (b, 2 * n_heads + h)),
                  pl.BlockSpec((None, None, n_pairs, GRID_W, 2 * GRID_W), lambda h, b: (jl, h, 0, 0, 0))],
        out_specs=pl.BlockSpec((t, HEAD_DIM), lambda h, b: (b, h)),
        out_shape=jax.ShapeDtypeStruct((m, d), BF16),
        scratch_shapes=[pltpu.VMEM((qb, n_keys), F32), pltpu.VMEM((qb, n_keys), F32),
                        pltpu.VMEM((qb, n_keys), BF16), pltpu.VMEM((qb, n_keys), BF16)],
        compiler_params=_cparams("parallel", "parallel"),
        name="na_attn",
    )(qkv, qkv, qkv, bias_pairs)


def _final_norm_kernel(x_ref, g_ref, o_ref):
    g = g_ref[...]

    def body(r, carry):
        rows = pl.ds(pl.multiple_of(r * NORM_CHUNK, NORM_CHUNK), NORM_CHUNK)
        xs = x_ref[rows, :]
        ms = jnp.mean(xs * xs, axis=-1, keepdims=True)
        o_ref[rows, :] = xs * lax.rsqrt(ms + EPS) * g
        return carry

    lax.fori_loop(0, x_ref.shape[0] // NORM_CHUNK, body, 0)


def _final_norm(x, g, *, batch, t, seq):
    d = x.shape[1]
    nblk = seq // CTX_ROWS
    return pl.pallas_call(
        _final_norm_kernel,
        grid=(batch, nblk),
        in_specs=[pl.BlockSpec((None, CTX_ROWS, d), lambda b, j: (b, 1 + j, 0)),
                  pl.BlockSpec((1, d), lambda b, j: (0, 0))],
        out_specs=pl.BlockSpec((None, CTX_ROWS, d), lambda b, j: (b, j, 0)),
        out_shape=jax.ShapeDtypeStruct((batch, seq, d), F32),
        compiler_params=_cparams("parallel", "parallel"),
        name="final_norm",
    )(x.reshape(batch, t, d), g)


def kernel(x, c, ctx, c_ctx, ada_w, ada_b, norm1_g, norm2_g, mlp_w1, mlp_w2, lru_w_in, lru_conv_w, lru_conv_b, lru_lambda, lru_wa, lru_ba, lru_wx, lru_bx, lru_w_out, na_w_qkv, na_rpb, na_w_o, final_g):
    batch, seq, d = x.shape
    ctx_len = ctx.shape[1]
    depth = ada_w.shape[0]
    assert ctx_len == CTX_ROWS and batch <= 4 and seq % (QROWS * GRID_W) == 0
    t = ctx_len + seq
    tpb = 3
    tm = t // tpb
    assert tm % CTX_ROWS == 0
    tf = min(1024, mlp_w1.shape[2])

    def col_tile(n, target):
        return max(c for c in range(LANES, target + 1, LANES) if n % c == 0)

    tn_in = col_tile(lru_w_in.shape[2], 1536)
    tn_qkv = col_tile(na_w_qkv.shape[2], 1536)
    tn_res = col_tile(d, 1024)

    xs = jnp.concatenate([ctx, x], axis=1).reshape(batch * t, d)
    c8 = jnp.zeros((8, d), F32).at[:batch].set(c).at[4].set(c_ctx)
    mods = _ada_mods(c8, ada_w, ada_b).reshape(depth, 8, N_MOD, 1, d)
    g1 = norm1_g.reshape(depth, 1, d)
    g2 = norm2_g.reshape(depth, 1, d)
    w1, w2 = mlp_w1.astype(BF16), mlp_w2.astype(BF16)
    w_in, w_out = lru_w_in.astype(BF16), lru_w_out.astype(BF16)
    w_qkv, w_o = na_w_qkv.astype(BF16), na_w_o.astype(BF16)
    pairs, tab = _na_window_plan(seq // GRID_W)
    bias_pairs = _na_bias_pairs(na_rpb, pairs)

    for i in range(depth):
        j = i // 2
        if i % 2 == 0:
            gu = _mm_norm(xs, g1, mods, i, w_in, j, F32, tm=tm, tn=tn_in, tpb=tpb)
            z = _lru_core(gu, lru_conv_w, lru_conv_b, lru_lambda, lru_wa, lru_ba, lru_wx, lru_bx, j,
                          batch=batch, t=t)
            xs = _mm_res(z, w_out, j, xs, mods, i, tm=tm, tn=tn_res, tpb=tpb)
        else:
            qkv = _mm_norm(xs, g1, mods, i, w_qkv, j, BF16, tm=tm, tn=tn_qkv, tpb=tpb)
            o = _na_attention(qkv, bias_pairs, tab, j, batch=batch, t=t)
            xs = _mm_res(o, w_o, j, xs, mods, i, tm=tm, tn=tn_res, tpb=tpb)
        xs = _mlp(xs, g2, mods, i, w1, w2, tm=tm, tf=tf, tpb=tpb)

    return _final_norm(xs, final_g.reshape(1, d), batch=batch, t=t, seq=seq)
```

```python
import functools
import math

import numpy as np
import jax
import jax.numpy as jnp
from jax import lax
from jax.experimental import pallas as pl
from jax.experimental.pallas import tpu as pltpu

GRID_W = 64
WIN_ROWS = 8
WIN_COLS = 16
HEAD_DIM = 128
LRU_BLOCK = 256
CONV_WIDTH = 4
LRU_C = 8.0
N_MOD = 6
EPS = 1e-6

LANES = 128
CTX_ROWS = 256
NORM_CHUNK = 32
QROWS = 4
KROWS = QROWS + WIN_ROWS
NA_CHUNK = 32
SEG_PITCH = LRU_BLOCK + 8
SCAN_UNROLL = 4
NEG = -1e30
VMEM_LIMIT = 56 * 1024 * 1024

F32 = jnp.float32
BF16 = jnp.bfloat16


def _cparams(*sem):
    return pltpu.CompilerParams(dimension_semantics=sem, vmem_limit_bytes=VMEM_LIMIT)


def _mod_spec(layer, k, row_fn, width, col_fn=None):
    if col_fn is None:
        return pl.BlockSpec((None, None, None, 1, width),
                            lambda *g: (layer, row_fn(*g), k, 0, 0))
    return pl.BlockSpec((None, None, None, 1, width),
                        lambda *g: (layer, row_fn(*g), k, 0, col_fn(*g)))


def _gain_spec(layer, d):
    return pl.BlockSpec((None, 1, d), lambda *g: (layer, 0, 0))


def _ada_kernel(c_ref, w_ref, b_ref, o_ref):
    c = c_ref[...]
    s = (c * jax.nn.sigmoid(c)).astype(BF16)
    o_ref[...] = jnp.dot(s, w_ref[...].astype(BF16), preferred_element_type=F32) + b_ref[...]


def _ada_mods(c8, ada_w, ada_b, tn=1024):
    depth, d, n = ada_w.shape
    tn = math.gcd(tn, n)
    return pl.pallas_call(
        _ada_kernel,
        grid=(depth, n // tn),
        in_specs=[pl.BlockSpec((8, d), lambda l, j: (0, 0)),
                  pl.BlockSpec((None, d, tn), lambda l, j: (l, 0, j)),
                  pl.BlockSpec((None, 1, tn), lambda l, j: (l, 0, j))],
        out_specs=pl.BlockSpec((None, 8, tn), lambda l, j: (l, 0, j)),
        out_shape=jax.ShapeDtypeStruct((depth, 8, n), F32),
        compiler_params=_cparams("parallel", "parallel"),
        name="ada_mods",
    )(c8, ada_w, ada_b.reshape(depth, 1, n))


def _norm_mod_rows(x_ref, g_ref, sh_ref, sc_ref, csh_ref, csc_ref, last, hn_ref, rinv_ref, ctx_rows):
    tm, d = x_ref.shape

    def rms_body(r, carry):
        rows = pl.ds(pl.multiple_of(r * NORM_CHUNK, NORM_CHUNK), NORM_CHUNK)
        xs = x_ref[rows, :]
        ms = jnp.mean(xs * xs, axis=-1, keepdims=True)
        rinv_ref[rows, :] = jnp.broadcast_to(lax.rsqrt(ms + EPS), (NORM_CHUNK, LANES))
        return carry

    lax.fori_loop(0, tm // NORM_CHUNK, rms_body, 0, unroll=4)

    g = g_ref[...]
    sh = sh_ref[...]
    gain = g * (1.0 + sc_ref[...])

    def scale_rows(lo, hi, gain_row, shift_row):
        def body(r, carry):
            rows = pl.ds(pl.multiple_of(r * NORM_CHUNK, NORM_CHUNK), NORM_CHUNK)
            rinv = rinv_ref[rows, :]
            for c in range(d // LANES):
                lanes = slice(c * LANES, (c + 1) * LANES)
                hn_ref[rows, lanes] = (x_ref[rows, lanes] * rinv * gain_row[:, lanes]
                                       + shift_row[:, lanes]).astype(hn_ref.dtype)
            return carry

        lax.fori_loop(lo, hi, body, 0, unroll=2)

    n_plain = (tm - ctx_rows) // NORM_CHUNK
    scale_rows(0, n_plain, gain, sh)
    if ctx_rows:
        sh_tail = jnp.where(last, csh_ref[...], sh)
        gain_tail = jnp.where(last, g * (1.0 + csc_ref[...]), gain)
        scale_rows(n_plain, tm // NORM_CHUNK, gain_tail, sh_tail)


def _mm_norm_kernel(x_ref, g_ref, sh_ref, sc_ref, csh_ref, csc_ref, w_ref, o_ref, hn_ref, rinv_ref,
                    *, tpb):
    i = pl.program_id(0)

    @pl.when(pl.program_id(1) == 0)
    def _():
        _norm_mod_rows(x_ref, g_ref, sh_ref, sc_ref, csh_ref, csc_ref, (i % tpb) == tpb - 1, hn_ref,
                       rinv_ref, CTX_ROWS)

    o_ref[...] = jnp.dot(hn_ref[...], w_ref[...], preferred_element_type=F32).astype(o_ref.dtype)


def _mm_norm(x, gains, mods, layer, w, wl, out_dtype, *, tm, tn, tpb):
    m, d = x.shape
    n = w.shape[2]
    row = lambda i, j: i // tpb
    ctx = lambda i, j: 4
    return pl.pallas_call(
        functools.partial(_mm_norm_kernel, tpb=tpb),
        grid=(m // tm, n // tn),
        in_specs=[pl.BlockSpec((tm, d), lambda i, j: (i, 0)),
                  _gain_spec(layer, d),
                  _mod_spec(layer, 0, row, d), _mod_spec(layer, 1, row, d),
                  _mod_spec(layer, 0, ctx, d), _mod_spec(layer, 1, ctx, d),
                  pl.BlockSpec((None, d, tn), lambda i, j: (wl, 0, j))],
        out_specs=pl.BlockSpec((tm, tn), lambda i, j: (i, j)),
        out_shape=jax.ShapeDtypeStruct((m, n), out_dtype),
        scratch_shapes=[pltpu.VMEM((tm, d), BF16), pltpu.VMEM((tm, LANES), F32)],
        compiler_params=_cparams("parallel", "arbitrary"),
        name="mm_norm",
    )(x, gains, mods, mods, mods, mods, w)


def _mm_res_kernel(a_ref, w_ref, r_ref, gt_ref, cgt_ref, o_ref, *, tpb, ctx_rows):
    acc = jnp.dot(a_ref[...], w_ref[...], preferred_element_type=F32)
    g = gt_ref[...]
    split = o_ref.shape[0] - ctx_rows
    o_ref[:split, :] = r_ref[:split, :] + g * acc[:split]
    if ctx_rows:
        g_tail = jnp.where((pl.program_id(1) % tpb) == tpb - 1, cgt_ref[...], g)
        o_ref[split:, :] = r_ref[split:, :] + g_tail * acc[split:]


def _mm_res(a, w, wl, res, mods, layer, *, tm, tn, tpb):
    m, k = a.shape
    n = w.shape[2]
    return pl.pallas_call(
        functools.partial(_mm_res_kernel, tpb=tpb, ctx_rows=CTX_ROWS),
        grid=(n // tn, m // tm),
        in_specs=[pl.BlockSpec((tm, k), lambda j, i: (i, 0)),
                  pl.BlockSpec((None, k, tn), lambda j, i: (wl, 0, j)),
                  pl.BlockSpec((tm, tn), lambda j, i: (i, j)),
                  _mod_spec(layer, 2, lambda j, i: i // tpb, tn, lambda j, i: j),
                  _mod_spec(layer, 2, lambda j, i: 4, tn, lambda j, i: j)],
        out_specs=pl.BlockSpec((tm, tn), lambda j, i: (i, j)),
        out_shape=jax.ShapeDtypeStruct((m, n), F32),
        compiler_params=_cparams("parallel", "parallel"),
        name="mm_res",
    )(a, w, res, mods, mods)


def _mm_res_latent(a, w, wl, res, mods, layer, *, batch, seq, tm, tn):
    k = a.shape[1]
    n = w.shape[2]
    t = a.shape[0] // batch
    tpb = seq // tm
    tile = lambda j, i: (i // tpb, i % tpb)
    return pl.pallas_call(
        functools.partial(_mm_res_kernel, tpb=tpb, ctx_rows=0),
        grid=(n // tn, batch * tpb),
        in_specs=[pl.BlockSpec((None, tm, k), lambda j, i: (*tile(j, i), 0)),
                  pl.BlockSpec((None, k, tn), lambda j, i: (wl, 0, j)),
                  pl.BlockSpec((None, tm, tn), lambda j, i: (*tile(j, i), j)),
                  _mod_spec(layer, 2, lambda j, i: i // tpb, tn, lambda j, i: j),
                  _mod_spec(layer, 2, lambda j, i: 4, tn, lambda j, i: j)],
        out_specs=pl.BlockSpec((None, tm, tn), lambda j, i: (*tile(j, i), j)),
        out_shape=jax.ShapeDtypeStruct((batch, seq, n), F32),
        compiler_params=_cparams("parallel", "parallel"),
        name="mm_res_latent",
    )(a.reshape(batch, t, k), w, res.reshape(batch, t, n), mods, mods)


def _mlp_kernel(x_ref, g_ref, sh_ref, sc_ref, gt_ref, csh_ref, csc_ref, cgt_ref, w1_ref, w2_ref, fg_ref,
                o_ref, hn_ref, rinv_ref, *, tpb, ctx_rows, final_norm):
    i = pl.program_id(0)
    f = pl.program_id(1)
    last = (i % tpb) == tpb - 1

    @pl.when(f == 0)
    def _():
        _norm_mod_rows(x_ref, g_ref, sh_ref, sc_ref, csh_ref, csc_ref, last, hn_ref, rinv_ref, ctx_rows)
        o_ref[...] = jnp.zeros_like(o_ref)

    a = jnp.dot(hn_ref[...], w1_ref[...], preferred_element_type=F32)
    a = jnp.square(jnp.maximum(a, 0.0)).astype(BF16)
    o_ref[...] += jnp.dot(a, w2_ref[...], preferred_element_type=F32)

    @pl.when(f == pl.num_programs(1) - 1)
    def _():
        g = gt_ref[...]
        tm = o_ref.shape[0]
        split = tm - ctx_rows
        if final_norm:
            fg = fg_ref[...]
            d = o_ref.shape[1]

            def resid_body(r, carry):
                rows = pl.ds(pl.multiple_of(r * NORM_CHUNK, NORM_CHUNK), NORM_CHUNK)
                y = x_ref[rows, :] + g * o_ref[rows, :]
                o_ref[rows, :] = y
                ms = jnp.mean(y * y, axis=-1, keepdims=True)
                rinv_ref[rows, :] = jnp.broadcast_to(lax.rsqrt(ms + EPS), (NORM_CHUNK, LANES))
                return carry

            lax.fori_loop(0, tm // NORM_CHUNK, resid_body, 0, unroll=4)

            def norm_body(r, carry):
                rows = pl.ds(pl.multiple_of(r * NORM_CHUNK, NORM_CHUNK), NORM_CHUNK)
                rinv = rinv_ref[rows, :]
                for c in range(d // LANES):
                    lanes = slice(c * LANES, (c + 1) * LANES)
                    o_ref[rows, lanes] = o_ref[rows, lanes] * rinv * fg[:, lanes]
                return carry

            lax.fori_loop(0, tm // NORM_CHUNK, norm_body, 0, unroll=2)
        else:
            o_ref[:split, :] = x_ref[:split, :] + g * o_ref[:split, :]
            if ctx_rows:
                g_tail = jnp.where(last, cgt_ref[...], g)
                o_ref[split:, :] = x_ref[split:, :] + g_tail * o_ref[split:, :]


def _mlp(x, gains, mods, layer, w1, w2, final_g, *, tm, tf, tpb, ctx_rows, final_norm):
    assert not (final_norm and ctx_rows)
    m, d = x.shape
    dff = w1.shape[2]
    row = lambda i, f: i // tpb
    ctx = lambda i, f: 4
    return pl.pallas_call(
        functools.partial(_mlp_kernel, tpb=tpb, ctx_rows=ctx_rows, final_norm=final_norm),
        grid=(m // tm, dff // tf),
        in_specs=[pl.BlockSpec((tm, d), lambda i, f: (i, 0)),
                  _gain_spec(layer, d),
                  _mod_spec(layer, 3, row, d), _mod_spec(layer, 4, row, d), _mod_spec(layer, 5, row, d),
                  _mod_spec(layer, 3, ctx, d), _mod_spec(layer, 4, ctx, d), _mod_spec(layer, 5, ctx, d),
                  pl.BlockSpec((None, d, tf), lambda i, f: (layer, 0, f)),
                  pl.BlockSpec((None, tf, d), lambda i, f: (layer, f, 0)),
                  pl.BlockSpec((1, d), lambda i, f: (0, 0))],
        out_specs=pl.BlockSpec((tm, d), lambda i, f: (i, 0)),
        out_shape=jax.ShapeDtypeStruct((m, d), F32),
        scratch_shapes=[pltpu.VMEM((tm, d), BF16), pltpu.VMEM((tm, LANES), F32)],
        compiler_params=_cparams("parallel", "arbitrary"),
        name="mlp",
    )(x, gains, mods, mods, mods, mods, mods, mods, w1, w2, final_g)


def _gelu_tanh(x):
    c = 0.7978845608028654
    hx = 0.5 * x
    return hx + hx * jnp.tanh(x * (c + (c * 0.044715) * (x * x)))


def _lru_kernel(gate_ref, u_ref, cw_ref, cb_ref, lam_ref, wa_ref, ba_ref, wx_ref, bx_ref, z_ref,
                upad, wcat, a_s, b_s, p_s, h_s, *, nseg):
    seg = LRU_BLOCK
    nh = LRU_BLOCK // LANES
    n_lat = (nseg - 1) * seg
    ctx0 = 8 + n_lat + 8
    zeros8 = jnp.zeros((8, LANES), F32)
    for l in range(nh):
        lanes = slice(l * LANES, (l + 1) * LANES)
        upad[l, 0:8, :] = zeros8
        upad[l, 8:8 + n_lat, :] = u_ref[0:n_lat, lanes]
        upad[l, 8 + n_lat:ctx0, :] = zeros8
        upad[l, ctx0:ctx0 + CTX_ROWS, :] = u_ref[n_lat:, lanes]
        upad[l, ctx0 + CTX_ROWS:ctx0 + CTX_ROWS + 8, :] = zeros8
    for d in range(2):
        wcat[:, (2 * d) * seg:(2 * d + 1) * seg] = (0.5 * wa_ref[d]).astype(BF16)
        wcat[:, (2 * d + 1) * seg:(2 * d + 2) * seg] = (0.5 * wx_ref[d]).astype(BF16)

    nlam = -lam_ref[...]
    hcoef = (-0.5 * LRU_C) * (jnp.maximum(nlam, 0.0) + jnp.log1p(jnp.exp(-jnp.abs(nlam))))
    cw = cw_ref[...]
    cb = cb_ref[...]
    hba = 0.5 * ba_ref[...]
    hbx = 0.5 * bx_ref[...]

    def coeff_body(s, carry):
        base = s * seg + jnp.where(s == nseg - 1, 16, 8)
        ucs = []
        for l in range(nh):
            lanes = slice(l * LANES, (l + 1) * LANES)
            acc = cb[:, lanes]
            for k in range(CONV_WIDTH):
                acc = acc + cw[k:k + 1, lanes] * upad[l, pl.ds(base + (k - CONV_WIDTH // 2), seg), :]
            ucs.append(acc)
        uc = jnp.concatenate(ucs, axis=-1)
        gts = jnp.dot(uc.astype(BF16), wcat[...], preferred_element_type=F32)
        row0 = pl.multiple_of(s * SEG_PITCH, 8)
        for l in range(nh):
            lanes = slice(l * LANES, (l + 1) * LANES)
            huc = 0.5 * ucs[l]
            for d in range(2):
                ga = gts[:, 2 * d * seg + l * LANES:2 * d * seg + (l + 1) * LANES]
                gx = gts[:, (2 * d + 1) * seg + l * LANES:(2 * d + 1) * seg + (l + 1) * LANES]
                hc = hcoef[d:d + 1, lanes]
                log_a = jnp.tanh(ga + hba[d, :, lanes]) * hc + hc
                a = jnp.exp(log_a)
                gated_u = (jnp.tanh(gx + hbx[d, :, lanes]) + 1.0) * huc
                bb = jnp.sqrt(jnp.tanh(log_a) * (-1.0 - a * a)) * gated_u
                a_s[d, l, pl.ds(row0, seg), :] = a
                b_s[d, l, pl.ds(row0, seg), :] = bb
        return carry

    lax.fori_loop(0, nseg, coeff_body, 0)

    def ld(ref, d, l, j):
        v8 = ref[d, l, pl.ds(j, 8, stride=SEG_PITCH), :]
        v1 = ref[d, l, pl.ds(j + 8 * SEG_PITCH, 1), :]
        return v8, v1

    def st(ref, d, l, j, v8, v1):
        ref[d, l, pl.ds(j, 8, stride=SEG_PITCH), :] = v8
        ref[d, l, pl.ds(j + 8 * SEG_PITCH, 1), :] = v1

    def scan_body(t, carry):
        out = []
        for d in range(2):
            j = t if d == 0 else seg - 1 - t
            for l in range(nh):
                h8, h1, p8, p1 = carry[d * nh + l]
                a8, a1 = ld(a_s, d, l, j)
                b8, b1 = ld(b_s, d, l, j)
                h8 = a8 * h8 + b8
                h1 = a1 * h1 + b1
                p8 = a8 * p8
                p1 = a1 * p1
                st(p_s, d, l, j, p8, p1)
                st(h_s, d, l, j, h8, h1)
                out.append((h8, h1, p8, p1))
        return tuple(out)

    init = tuple((jnp.zeros((8, LANES), F32), jnp.zeros((1, LANES), F32),
                  jnp.ones((8, LANES), F32), jnp.ones((1, LANES), F32)) for _ in range(2 * nh))
    fin = lax.fori_loop(0, seg, scan_body, init, unroll=SCAN_UNROLL)

    def seg_row(v8, v1, s):
        return v1 if s == 8 else v8[s:s + 1, :]

    carries = []
    for d in range(2):
        per_l = []
        for l in range(nh):
            h8, h1, p8, p1 = fin[d * nh + l]
            latent = list(range(nseg - 1))
            order = [nseg - 1] + (latent if d == 0 else latent[::-1])
            c = jnp.zeros((1, LANES), F32)
            cs = {}
            for s in order:
                cs[s] = c
                c = seg_row(h8, h1, s) + seg_row(p8, p1, s) * c
            per_l.append(cs)
        carries.append(per_l)

    for s in range(nseg):
        rows = slice(s * seg, (s + 1) * seg)
        srows = slice(s * SEG_PITCH, s * SEG_PITCH + seg)
        for l in range(nh):
            lanes = slice(l * LANES, (l + 1) * LANES)
            y = (h_s[0, l, srows, :] + p_s[0, l, srows, :] * carries[0][l][s]
                 + h_s[1, l, srows, :] + p_s[1, l, srows, :] * carries[1][l][s])
            z_ref[rows, lanes] = (_gelu_tanh(gate_ref[rows, lanes]) * y).astype(z_ref.dtype)


def _lru_core(gu, conv_w, conv_b, lam, wa, ba, wx, bx, jl, *, batch, t):
    m = gu.shape[0]
    width = gu.shape[1] // 2
    nb = width // LRU_BLOCK
    nseg = t // LRU_BLOCK
    assert nseg == 9, "scan kernel keeps 8 + 1 time segments on sublanes"
    nh = LRU_BLOCK // LANES
    srows = nseg * SEG_PITCH
    nl = conv_w.shape[0]
    wspec = pl.BlockSpec((None, 2, None, LRU_BLOCK, LRU_BLOCK), lambda b, n: (jl, 0, n, 0, 0))
    bspec = pl.BlockSpec((None, 2, None, 1, LRU_BLOCK), lambda b, n: (jl, 0, n, 0, 0))
    return pl.pallas_call(
        functools.partial(_lru_kernel, nseg=nseg),
        grid=(batch, nb),
        in_specs=[pl.BlockSpec((t, LRU_BLOCK), lambda b, n: (b, n)),
                  pl.BlockSpec((t, LRU_BLOCK), lambda b, n: (b, nb + n)),
                  pl.BlockSpec((None, CONV_WIDTH, LRU_BLOCK), lambda b, n: (jl, 0, n)),
                  pl.BlockSpec((None, 1, LRU_BLOCK), lambda b, n: (jl, 0, n)),
                  pl.BlockSpec((None, 2, LRU_BLOCK), lambda b, n: (jl, 0, n)),
                  wspec, bspec, wspec, bspec],
        out_specs=pl.BlockSpec((t, LRU_BLOCK), lambda b, n: (b, n)),
        out_shape=jax.ShapeDtypeStruct((m, width), BF16),
        scratch_shapes=[pltpu.VMEM((nh, t + 24, LANES), F32),
                        pltpu.VMEM((LRU_BLOCK, 4 * LRU_BLOCK), BF16),
                        pltpu.VMEM((2, nh, srows, LANES), F32),
                        pltpu.VMEM((2, nh, srows, LANES), F32),
                        pltpu.VMEM((2, nh, srows, LANES), F32),
                        pltpu.VMEM((2, nh, srows, LANES), F32)],
        compiler_params=_cparams("parallel", "parallel"),
        name="lru_core",
    )(gu, gu, conv_w, conv_b.reshape(nl, 1, width), lam, wa, ba.reshape(nl, 2, nb, 1, LRU_BLOCK),
      wx, bx.reshape(nl, 2, nb, 1, LRU_BLOCK))


_NT = (((1,), (1,)), ((), ()))
_EXP2_SCALE = (HEAD_DIM ** -0.5) * math.log2(math.e)


def _softmax_rows(parts):
    m = parts[0].max(axis=-1, keepdims=True)
    for p in parts[1:]:
        m = jnp.maximum(m, p.max(axis=-1, keepdims=True))
    es = [jnp.exp2((p - m) * _EXP2_SCALE) for p in parts]
    tot = es[0].sum(axis=-1, keepdims=True)
    for e in es[1:]:
        tot = tot + e.sum(axis=-1, keepdims=True)
    inv = 1.0 / tot
    return [(e * inv).astype(BF16) for e in es]


def _na_kernel(q_ref, k_ref, v_ref, pairs_ref, o_ref, s_a, s_b, p_a, p_b, *, n_qblocks, tab):
    qb = QROWS * GRID_W
    ctx_rows = slice(n_qblocks * qb, n_qblocks * qb + CTX_ROWS)
    kc = k_ref[ctx_rows, :]
    vc = v_ref[ctx_rows, :]

    s_c = lax.dot_general(q_ref[ctx_rows, :], kc, _NT, preferred_element_type=F32)
    (p_c,) = _softmax_rows([s_c])
    o_ref[ctx_rows, :] = jnp.dot(p_c, vc, preferred_element_type=F32).astype(o_ref.dtype)

    n_rows = n_qblocks * QROWS
    nloc = KROWS * GRID_W

    def q_rows(jj):
        return pl.ds(pl.multiple_of(jj * qb, qb), qb)

    def window(jj):
        ws = jnp.clip(jj * QROWS - WIN_ROWS // 2, 0, n_rows - KROWS)
        return pl.ds(pl.multiple_of(ws * GRID_W, QROWS * GRID_W), nloc)

    def logits(jj, s_ref):
        q = q_ref[q_rows(jj), :]
        s_ref[:, :nloc] = lax.dot_general(q, k_ref[window(jj), :], _NT, preferred_element_type=F32)
        s_ref[:, nloc:] = lax.dot_general(q, kc, _NT, preferred_element_type=F32)

    def probs(jj, s_ref, p_ref):
        var = jnp.where(jj == 0, 0, jnp.where(jj == n_qblocks - 1, 2, 1))
        for c in range(qb // NA_CHUNK):
            a, off = divmod(c * NA_CHUNK, GRID_W)
            rows = slice(c * NA_CHUNK, (c + 1) * NA_CHUNK)
            tiles = []
            for m in range(KROWS // 2):
                e0, e1, e2 = (tab[v][a][m] for v in range(3))
                e = e1 if e0 == e1 == e2 else jnp.where(var == 0, e0, jnp.where(var == 1, e1, e2))
                tiles.append(pairs_ref[e, off:off + NA_CHUNK, :])
            s_loc = s_ref[rows, :nloc] + jnp.concatenate(tiles, axis=-1)
            p_loc, p_ctx = _softmax_rows([s_loc, s_ref[rows, nloc:]])
            p_ref[rows, :nloc] = p_loc
            p_ref[rows, nloc:] = p_ctx

    def attend(jj, p_ref):
        o = (jnp.dot(p_ref[:, :nloc], v_ref[window(jj), :], preferred_element_type=F32)
             + jnp.dot(p_ref[:, nloc:], vc, preferred_element_type=F32))
        o_ref[q_rows(jj), :] = o.astype(o_ref.dtype)

    logits(0, s_a)

    def body(it, carry):
        j0 = 2 * it
        j1 = j0 + 1
        j2 = jnp.minimum(j0 + 2, n_qblocks - 1)
        logits(j1, s_b)
        probs(j0, s_a, p_a)
        attend(j0, p_a)
        probs(j1, s_b, p_b)
        logits(j2, s_a)
        attend(j1, p_b)
        return carry

    lax.fori_loop(0, n_qblocks // 2, body, 0)


def _na_window_plan(n_rows):
    n_dr = 2 * WIN_ROWS - 1
    idx = np.full((3, QROWS, KROWS), n_dr, np.int32)
    n_qblocks = n_rows // QROWS
    for v, jj in enumerate((0, 1, n_qblocks - 1)):
        r0 = jj * QROWS
        ws = int(np.clip(r0 - WIN_ROWS // 2, 0, n_rows - KROWS))
        for a in range(QROWS):
            qr = r0 + a
            rs = int(np.clip(qr - WIN_ROWS // 2, 0, n_rows - WIN_ROWS))
            for kr in range(KROWS):
                kra = ws + kr
                if rs <= kra < rs + WIN_ROWS:
                    idx[v, a, kr] = kra - qr + WIN_ROWS - 1
    pairs = sorted({(int(idx[v, a, 2 * m]), int(idx[v, a, 2 * m + 1]))
                    for v in range(3) for a in range(QROWS) for m in range(KROWS // 2)})
    tab = tuple(tuple(tuple(pairs.index((int(idx[v, a, 2 * m]), int(idx[v, a, 2 * m + 1])))
                            for m in range(KROWS // 2)) for a in range(QROWS)) for v in range(3))
    return pairs, tab


def _na_bias_pairs(rpb, pairs):
    nl, n_heads = rpb.shape[:2]
    qc = np.arange(GRID_W)[:, None]
    kc = np.arange(GRID_W)[None, :]
    cs = np.clip(qc - WIN_COLS // 2, 0, GRID_W - WIN_COLS)
    col_ok = (kc >= cs) & (kc < cs + WIN_COLS)
    col_off = np.where(col_ok, kc - qc + WIN_COLS - 1, 0)
    onehot = (np.arange(2 * WIN_COLS - 1)[:, None, None] == col_off[None]) & col_ok[None]
    onehot = jnp.asarray(onehot.astype(np.float32))
    t2 = jnp.einsum('lhrm,mqk->lhrqk', rpb.astype(F32) * (HEAD_DIM ** 0.5), onehot,
                    precision=lax.Precision.HIGHEST)
    t2 = jnp.where(jnp.asarray(col_ok)[None, None, None], t2, NEG)
    masked = jnp.full((nl, n_heads, GRID_W, GRID_W), NEG, F32)
    rows = [t2[:, :, r] for r in range(2 * WIN_ROWS - 1)] + [masked]
    tiles = [jnp.concatenate([rows[left], rows[right]], axis=-1) for left, right in pairs]
    return jnp.stack(tiles, axis=2)


def _na_attention(qkv, bias_pairs, tab, jl, *, batch, t):
    m = qkv.shape[0]
    d = qkv.shape[1] // 3
    n_heads = d // HEAD_DIM
    qb = QROWS * GRID_W
    n_qblocks = (t - CTX_ROWS) // qb
    n_pairs = bias_pairs.shape[2]
    n_keys = KROWS * GRID_W + CTX_ROWS
    assert n_qblocks % 2 == 0
    return pl.pallas_call(
        functools.partial(_na_kernel, n_qblocks=n_qblocks, tab=tab),
        grid=(n_heads, batch),
        in_specs=[pl.BlockSpec((t, HEAD_DIM), lambda h, b: (b, h)),
                  pl.BlockSpec((t, HEAD_DIM), lambda h, b: (b, n_heads + h)),
                  pl.BlockSpec((t, HEAD_DIM), lambda h, b: (b, 2 * n_heads + h)),
                  pl.BlockSpec((None, None, n_pairs, GRID_W, 2 * GRID_W), lambda h, b: (jl, h, 0, 0, 0))],
        out_specs=pl.BlockSpec((t, HEAD_DIM), lambda h, b: (b, h)),
        out_shape=jax.ShapeDtypeStruct((m, d), BF16),
        scratch_shapes=[pltpu.VMEM((qb, n_keys), F32), pltpu.VMEM((qb, n_keys), F32),
                        pltpu.VMEM((qb, n_keys), BF16), pltpu.VMEM((qb, n_keys), BF16)],
        compiler_params=_cparams("parallel", "parallel"),
        name="na_attn",
    )(qkv, qkv, qkv, bias_pairs)


def kernel(x, c, ctx, c_ctx, ada_w, ada_b, norm1_g, norm2_g, mlp_w1, mlp_w2, lru_w_in, lru_conv_w, lru_conv_b, lru_lambda, lru_wa, lru_ba, lru_wx, lru_bx, lru_w_out, na_w_qkv, na_rpb, na_w_o, final_g):
    batch, seq, d = x.shape
    ctx_len = ctx.shape[1]
    depth = ada_w.shape[0]
    assert ctx_len == CTX_ROWS and batch <= 4 and seq % (QROWS * GRID_W) == 0
    t = ctx_len + seq
    tpb = 3
    tm = t // tpb
    tpb_mm = 2
    tm_mm = t // tpb_mm
    assert tm % CTX_ROWS == 0 and tm_mm % LANES == 0 and tm_mm >= CTX_ROWS
    tm_lat_res = min(1024, seq)
    tm_lat = min(512, seq)
    tf = min(1024, mlp_w1.shape[2])

    def col_tile(n, target):
        return max(c for c in range(LANES, target + 1, LANES) if n % c == 0)

    tn_in = col_tile(lru_w_in.shape[2], 1536)
    tn_qkv = col_tile(na_w_qkv.shape[2], 1536)
    tn_res = col_tile(d, 1024)

    xs = jnp.concatenate([x, ctx], axis=1).reshape(batch * t, d)
    fg = final_g.reshape(1, d)
    c8 = jnp.zeros((8, d), F32).at[:batch].set(c).at[4].set(c_ctx)
    mods = _ada_mods(c8, ada_w, ada_b).reshape(depth, 8, N_MOD, 1, d)
    g1 = norm1_g.reshape(depth, 1, d)
    g2 = norm2_g.reshape(depth, 1, d)
    w1, w2 = mlp_w1.astype(BF16), mlp_w2.astype(BF16)
    w_in, w_out = lru_w_in.astype(BF16), lru_w_out.astype(BF16)
    w_qkv, w_o = na_w_qkv.astype(BF16), na_w_o.astype(BF16)
    pairs, tab = _na_window_plan(seq // GRID_W)
    bias_pairs = _na_bias_pairs(na_rpb, pairs)

    for i in range(depth):
        j = i // 2
        if i % 2 == 0:
            gu = _mm_norm(xs, g1, mods, i, w_in, j, F32, tm=tm_mm, tn=tn_in, tpb=tpb_mm)
            mixed = _lru_core(gu, lru_conv_w, lru_conv_b, lru_lambda, lru_wa, lru_ba, lru_wx, lru_bx, j,
                              batch=batch, t=t)
            w_mix = w_out
        else:
            qkv = _mm_norm(xs, g1, mods, i, w_qkv, j, BF16, tm=tm_mm, tn=tn_qkv, tpb=tpb_mm)
            mixed = _na_attention(qkv, bias_pairs, tab, j, batch=batch, t=t)
            w_mix = w_o
        if i < depth - 1:
            xs = _mm_res(mixed, w_mix, j, xs, mods, i, tm=tm_mm, tn=tn_res, tpb=tpb_mm)
            xs = _mlp(xs, g2, mods, i, w1, w2, fg, tm=tm, tf=tf, tpb=tpb, ctx_rows=CTX_ROWS,
                      final_norm=False)
        else:
            x_lat = _mm_res_latent(mixed, w_mix, j, xs, mods, i, batch=batch, seq=seq, tm=tm_lat_res,
                                   tn=tn_res)
            out = _mlp(x_lat.reshape(batch * seq, d), g2, mods, i, w1, w2, fg, tm=tm_lat, tf=tf,
                       tpb=seq // tm_lat, ctx_rows=0, final_norm=True)
    return out.reshape(batch, seq, d)
```

```python
import functools
import math

import numpy as np
import jax
import jax.numpy as jnp
from jax import lax
from jax.experimental import pallas as pl
from jax.experimental.pallas import tpu as pltpu

GRID_W = 64
WIN_ROWS = 8
WIN_COLS = 16
HEAD_DIM = 128
LRU_BLOCK = 256
CONV_WIDTH = 4
LRU_C = 8.0
N_MOD = 6
EPS = 1e-6

LANES = 128
CTX_ROWS = 256
NORM_CHUNK = 32
QROWS = 4
KROWS = QROWS + WIN_ROWS
NA_CHUNK = 32
SEG_PITCH = LRU_BLOCK + 8
SCAN_UNROLL = 4
NEG = -1e30
VMEM_LIMIT = 56 * 1024 * 1024

F32 = jnp.float32
BF16 = jnp.bfloat16


def _cparams(*sem):
    return pltpu.CompilerParams(dimension_semantics=sem, vmem_limit_bytes=VMEM_LIMIT)


def _mod_spec(layer, k, row_fn, width, col_fn=None):
    if col_fn is None:
        return pl.BlockSpec((None, None, None, 1, width),
                            lambda *g: (layer, row_fn(*g), k, 0, 0))
    return pl.BlockSpec((None, None, None, 1, width),
                        lambda *g: (layer, row_fn(*g), k, 0, col_fn(*g)))


def _gain_spec(layer, d):
    return pl.BlockSpec((None, 1, d), lambda *g: (layer, 0, 0))


def _ada_kernel(c_ref, w_ref, b_ref, o_ref):
    c = c_ref[...]
    s = (c * jax.nn.sigmoid(c)).astype(BF16)
    o_ref[...] = jnp.dot(s, w_ref[...].astype(BF16), preferred_element_type=F32) + b_ref[...]


def _ada_mods(c8, ada_w, ada_b, tn=2048):
    depth, d, n = ada_w.shape
    tn = math.gcd(tn, n)
    return pl.pallas_call(
        _ada_kernel,
        grid=(depth, n // tn),
        in_specs=[pl.BlockSpec((8, d), lambda l, j: (0, 0)),
                  pl.BlockSpec((None, d, tn), lambda l, j: (l, 0, j)),
                  pl.BlockSpec((None, 1, tn), lambda l, j: (l, 0, j))],
        out_specs=pl.BlockSpec((None, 8, tn), lambda l, j: (l, 0, j)),
        out_shape=jax.ShapeDtypeStruct((depth, 8, n), F32),
        compiler_params=_cparams("parallel", "parallel"),
        name="ada_mods",
    )(c8, ada_w, ada_b.reshape(depth, 1, n))


def _norm_mod_rows(x_ref, g_ref, sh_ref, sc_ref, csh_ref, csc_ref, last, hn_ref, rinv_ref, ctx_rows):
    tm, d = x_ref.shape

    def rms_body(r, carry):
        rows = pl.ds(pl.multiple_of(r * NORM_CHUNK, NORM_CHUNK), NORM_CHUNK)
        xs = x_ref[rows, :]
        ms = jnp.mean(xs * xs, axis=-1, keepdims=True)
        rinv_ref[rows, :] = jnp.broadcast_to(lax.rsqrt(ms + EPS), (NORM_CHUNK, LANES))
        return carry

    lax.fori_loop(0, tm // NORM_CHUNK, rms_body, 0, unroll=4)

    g = g_ref[...]
    sh = sh_ref[...]
    gain = g * (1.0 + sc_ref[...])

    def scale_rows(lo, hi, gain_row, shift_row):
        def body(r, carry):
            rows = pl.ds(pl.multiple_of(r * NORM_CHUNK, NORM_CHUNK), NORM_CHUNK)
            rinv = rinv_ref[rows, :]
            for c in range(d // LANES):
                lanes = slice(c * LANES, (c + 1) * LANES)
                hn_ref[rows, lanes] = (x_ref[rows, lanes] * rinv * gain_row[:, lanes]
                                       + shift_row[:, lanes]).astype(hn_ref.dtype)
            return carry

        lax.fori_loop(lo, hi, body, 0, unroll=2)

    n_plain = (tm - ctx_rows) // NORM_CHUNK
    scale_rows(0, n_plain, gain, sh)
    if ctx_rows:
        sh_tail = jnp.where(last, csh_ref[...], sh)
        gain_tail = jnp.where(last, g * (1.0 + csc_ref[...]), gain)
        scale_rows(n_plain, tm // NORM_CHUNK, gain_tail, sh_tail)


def _mm_norm_kernel(x_ref, g_ref, sh_ref, sc_ref, csh_ref, csc_ref, w_ref, o_ref, hn_ref, rinv_ref,
                    *, tpb):
    i = pl.program_id(0)

    @pl.when(pl.program_id(1) == 0)
    def _():
        _norm_mod_rows(x_ref, g_ref, sh_ref, sc_ref, csh_ref, csc_ref, (i % tpb) == tpb - 1, hn_ref,
                       rinv_ref, CTX_ROWS)

    o_ref[...] = jnp.dot(hn_ref[...], w_ref[...], preferred_element_type=F32).astype(o_ref.dtype)


def _mm_norm(x, gains, mods, layer, w, wl, out_dtype, *, tm, tn, tpb):
    m, d = x.shape
    n = w.shape[2]
    row = lambda i, j: i // tpb
    ctx = lambda i, j: 4
    return pl.pallas_call(
        functools.partial(_mm_norm_kernel, tpb=tpb),
        grid=(m // tm, n // tn),
        in_specs=[pl.BlockSpec((tm, d), lambda i, j: (i, 0)),
                  _gain_spec(layer, d),
                  _mod_spec(layer, 0, row, d), _mod_spec(layer, 1, row, d),
                  _mod_spec(layer, 0, ctx, d), _mod_spec(layer, 1, ctx, d),
                  pl.BlockSpec((None, d, tn), lambda i, j: (wl, 0, j))],
        out_specs=pl.BlockSpec((tm, tn), lambda i, j: (i, j)),
        out_shape=jax.ShapeDtypeStruct((m, n), out_dtype),
        scratch_shapes=[pltpu.VMEM((tm, d), BF16), pltpu.VMEM((tm, LANES), F32)],
        compiler_params=_cparams("parallel", "arbitrary"),
        name="mm_norm",
    )(x, gains, mods, mods, mods, mods, w)


def _mm_res_kernel(a_ref, w_ref, r_ref, gt_ref, cgt_ref, o_ref, *, tpb, ctx_rows):
    acc = jnp.dot(a_ref[...], w_ref[...], preferred_element_type=F32)
    g = gt_ref[...]
    split = o_ref.shape[0] - ctx_rows
    o_ref[:split, :] = r_ref[:split, :] + g * acc[:split]
    if ctx_rows:
        g_tail = jnp.where((pl.program_id(1) % tpb) == tpb - 1, cgt_ref[...], g)
        o_ref[split:, :] = r_ref[split:, :] + g_tail * acc[split:]


def _mm_res(a, w, wl, res, mods, layer, *, tm, tn, tpb):
    m, k = a.shape
    n = w.shape[2]
    return pl.pallas_call(
        functools.partial(_mm_res_kernel, tpb=tpb, ctx_rows=CTX_ROWS),
        grid=(n // tn, m // tm),
        in_specs=[pl.BlockSpec((tm, k), lambda j, i: (i, 0)),
                  pl.BlockSpec((None, k, tn), lambda j, i: (wl, 0, j)),
                  pl.BlockSpec((tm, tn), lambda j, i: (i, j)),
                  _mod_spec(layer, 2, lambda j, i: i // tpb, tn, lambda j, i: j),
                  _mod_spec(layer, 2, lambda j, i: 4, tn, lambda j, i: j)],
        out_specs=pl.BlockSpec((tm, tn), lambda j, i: (i, j)),
        out_shape=jax.ShapeDtypeStruct((m, n), F32),
        compiler_params=_cparams("parallel", "parallel"),
        name="mm_res",
    )(a, w, res, mods, mods)


def _mm_res_latent(a, w, wl, res, mods, layer, *, batch, seq, tm, tn):
    k = a.shape[1]
    n = w.shape[2]
    tpb = seq // tm
    tile = lambda j, i: (i // tpb, i % tpb)
    return pl.pallas_call(
        functools.partial(_mm_res_kernel, tpb=tpb, ctx_rows=0),
        grid=(n // tn, batch * tpb),
        in_specs=[pl.BlockSpec((None, tm, k), lambda j, i: (*tile(j, i), 0)),
                  pl.BlockSpec((None, k, tn), lambda j, i: (wl, 0, j)),
                  pl.BlockSpec((None, tm, tn), lambda j, i: (*tile(j, i), j)),
                  _mod_spec(layer, 2, lambda j, i: i // tpb, tn, lambda j, i: j),
                  _mod_spec(layer, 2, lambda j, i: 4, tn, lambda j, i: j)],
        out_specs=pl.BlockSpec((None, tm, tn), lambda j, i: (*tile(j, i), j)),
        out_shape=jax.ShapeDtypeStruct((batch, seq, n), F32),
        compiler_params=_cparams("parallel", "parallel"),
        name="mm_res_latent",
    )(a.reshape(batch, a.shape[0] // batch, k), w, res.reshape(batch, res.shape[0] // batch, n), mods, mods)


def _mlp_kernel(x_ref, g_ref, sh_ref, sc_ref, gt_ref, csh_ref, csc_ref, cgt_ref, w1_ref, w2_ref, fg_ref,
                o_ref, hn_ref, rinv_ref, *, tpb, ctx_rows, final_norm):
    i = pl.program_id(0)
    f = pl.program_id(1)
    last = (i % tpb) == tpb - 1

    @pl.when(f == 0)
    def _():
        _norm_mod_rows(x_ref, g_ref, sh_ref, sc_ref, csh_ref, csc_ref, last, hn_ref, rinv_ref, ctx_rows)
        o_ref[...] = jnp.zeros_like(o_ref)

    a = jnp.dot(hn_ref[...], w1_ref[...], preferred_element_type=F32)
    a = jnp.square(jnp.maximum(a, 0.0)).astype(BF16)
    o_ref[...] += jnp.dot(a, w2_ref[...], preferred_element_type=F32)

    @pl.when(f == pl.num_programs(1) - 1)
    def _():
        g = gt_ref[...]
        tm = o_ref.shape[0]
        split = tm - ctx_rows
        if final_norm:
            fg = fg_ref[...]
            d = o_ref.shape[1]

            def resid_body(r, carry):
                rows = pl.ds(pl.multiple_of(r * NORM_CHUNK, NORM_CHUNK), NORM_CHUNK)
                y = x_ref[rows, :] + g * o_ref[rows, :]
                o_ref[rows, :] = y
                ms = jnp.mean(y * y, axis=-1, keepdims=True)
                rinv_ref[rows, :] = jnp.broadcast_to(lax.rsqrt(ms + EPS), (NORM_CHUNK, LANES))
                return carry

            lax.fori_loop(0, tm // NORM_CHUNK, resid_body, 0, unroll=4)

            def norm_body(r, carry):
                rows = pl.ds(pl.multiple_of(r * NORM_CHUNK, NORM_CHUNK), NORM_CHUNK)
                rinv = rinv_ref[rows, :]
                for c in range(d // LANES):
                    lanes = slice(c * LANES, (c + 1) * LANES)
                    o_ref[rows, lanes] = o_ref[rows, lanes] * rinv * fg[:, lanes]
                return carry

            lax.fori_loop(0, tm // NORM_CHUNK, norm_body, 0, unroll=2)
        else:
            o_ref[:split, :] = x_ref[:split, :] + g * o_ref[:split, :]
            if ctx_rows:
                g_tail = jnp.where(last, cgt_ref[...], g)
                o_ref[split:, :] = x_ref[split:, :] + g_tail * o_ref[split:, :]


def _mlp(x, gains, mods, layer, w1, w2, final_g, *, tm, tf, tpb, ctx_rows, final_norm):
    assert not (final_norm and ctx_rows)
    m, d = x.shape
    dff = w1.shape[2]
    row = lambda i, f: i // tpb
    ctx = lambda i, f: 4
    return pl.pallas_call(
        functools.partial(_mlp_kernel, tpb=tpb, ctx_rows=ctx_rows, final_norm=final_norm),
        grid=(m // tm, dff // tf),
        in_specs=[pl.BlockSpec((tm, d), lambda i, f: (i, 0)),
                  _gain_spec(layer, d),
                  _mod_spec(layer, 3, row, d), _mod_spec(layer, 4, row, d), _mod_spec(layer, 5, row, d),
                  _mod_spec(layer, 3, ctx, d), _mod_spec(layer, 4, ctx, d), _mod_spec(layer, 5, ctx, d),
                  pl.BlockSpec((None, d, tf), lambda i, f: (layer, 0, f)),
                  pl.BlockSpec((None, tf, d), lambda i, f: (layer, f, 0)),
                  pl.BlockSpec((1, d), lambda i, f: (0, 0))],
        out_specs=pl.BlockSpec((tm, d), lambda i, f: (i, 0)),
        out_shape=jax.ShapeDtypeStruct((m, d), F32),
        scratch_shapes=[pltpu.VMEM((tm, d), BF16), pltpu.VMEM((tm, LANES), F32)],
        compiler_params=_cparams("parallel", "arbitrary"),
        name="mlp",
    )(x, gains, mods, mods, mods, mods, mods, mods, w1, w2, final_g)


def _gelu_tanh(x):
    c = 0.7978845608028654
    hx = 0.5 * x
    return hx + hx * jnp.tanh(x * (c + (c * 0.044715) * (x * x)))


def _lru_kernel(gate_ref, u_ref, cw_ref, cb_ref, lam_ref, wa_ref, ba_ref, wx_ref, bx_ref, z_ref,
                upad, wcat, a_s, b_s, p_s, h_s, *, nseg):
    seg = LRU_BLOCK
    nh = LRU_BLOCK // LANES
    n_lat = (nseg - 1) * seg
    ctx0 = 8 + n_lat + 8
    zeros8 = jnp.zeros((8, LANES), F32)
    for l in range(nh):
        lanes = slice(l * LANES, (l + 1) * LANES)
        upad[l, 0:8, :] = zeros8
        upad[l, 8:8 + n_lat, :] = u_ref[0:n_lat, lanes]
        upad[l, 8 + n_lat:ctx0, :] = zeros8
        upad[l, ctx0:ctx0 + CTX_ROWS, :] = u_ref[n_lat:, lanes]
        upad[l, ctx0 + CTX_ROWS:ctx0 + CTX_ROWS + 8, :] = zeros8
    for d in range(2):
        wcat[:, (2 * d) * seg:(2 * d + 1) * seg] = (0.5 * wa_ref[d]).astype(BF16)
        wcat[:, (2 * d + 1) * seg:(2 * d + 2) * seg] = (0.5 * wx_ref[d]).astype(BF16)

    nlam = -lam_ref[...]
    hcoef = (-0.5 * LRU_C) * (jnp.maximum(nlam, 0.0) + jnp.log1p(jnp.exp(-jnp.abs(nlam))))
    cw = cw_ref[...]
    cb = cb_ref[...]
    hba = 0.5 * ba_ref[...]
    hbx = 0.5 * bx_ref[...]

    def coeff_body(s, carry):
        base = s * seg + jnp.where(s == nseg - 1, 16, 8)
        ucs = []
        for l in range(nh):
            lanes = slice(l * LANES, (l + 1) * LANES)
            acc = cb[:, lanes]
            for k in range(CONV_WIDTH):
                acc = acc + cw[k:k + 1, lanes] * upad[l, pl.ds(base + (k - CONV_WIDTH // 2), seg), :]
            ucs.append(acc)
        uc = jnp.concatenate(ucs, axis=-1)
        gts = jnp.dot(uc.astype(BF16), wcat[...], preferred_element_type=F32)
        row0 = pl.multiple_of(s * SEG_PITCH, 8)
        for l in range(nh):
            lanes = slice(l * LANES, (l + 1) * LANES)
            huc = 0.5 * ucs[l]
            for d in range(2):
                ga = gts[:, 2 * d * seg + l * LANES:2 * d * seg + (l + 1) * LANES]
                gx = gts[:, (2 * d + 1) * seg + l * LANES:(2 * d + 1) * seg + (l + 1) * LANES]
                hc = hcoef[d:d + 1, lanes]
                log_a = jnp.tanh(ga + hba[d, :, lanes]) * hc + hc
                a = jnp.exp(log_a)
                gated_u = (jnp.tanh(gx + hbx[d, :, lanes]) + 1.0) * huc
                bb = jnp.sqrt(jnp.tanh(log_a) * (-1.0 - a * a)) * gated_u
                a_s[d, l, pl.ds(row0, seg), :] = a
                b_s[d, l, pl.ds(row0, seg), :] = bb
        return carry

    lax.fori_loop(0, nseg, coeff_body, 0, unroll=3)

    def ld(ref, d, l, j):
        v8 = ref[d, l, pl.ds(j, 8, stride=SEG_PITCH), :]
        v1 = ref[d, l, pl.ds(j + 8 * SEG_PITCH, 1), :]
        return v8, v1

    def st(ref, d, l, j, v8, v1):
        ref[d, l, pl.ds(j, 8, stride=SEG_PITCH), :] = v8
        ref[d, l, pl.ds(j + 8 * SEG_PITCH, 1), :] = v1

    def scan_body(t, carry):
        out = []
        for d in range(2):
            j = t if d == 0 else seg - 1 - t
            for l in range(nh):
                h8, h1, p8, p1 = carry[d * nh + l]
                a8, a1 = ld(a_s, d, l, j)
                b8, b1 = ld(b_s, d, l, j)
                h8 = a8 * h8 + b8
                h1 = a1 * h1 + b1
                p8 = a8 * p8
                p1 = a1 * p1
                st(p_s, d, l, j, p8, p1)
                st(h_s, d, l, j, h8, h1)
                out.append((h8, h1, p8, p1))
        return tuple(out)

    init = tuple((jnp.zeros((8, LANES), F32), jnp.zeros((1, LANES), F32),
                  jnp.ones((8, LANES), F32), jnp.ones((1, LANES), F32)) for _ in range(2 * nh))
    fin = lax.fori_loop(0, seg, scan_body, init, unroll=SCAN_UNROLL)

    def seg_row(v8, v1, s):
        return v1 if s == 8 else v8[s:s + 1, :]

    carries = []
    for d in range(2):
        per_l = []
        for l in range(nh):
            h8, h1, p8, p1 = fin[d * nh + l]
            latent = list(range(nseg - 1))
            order = [nseg - 1] + (latent if d == 0 else latent[::-1])
            c = jnp.zeros((1, LANES), F32)
            cs = {}
            for s in order:
                cs[s] = c
                c = seg_row(h8, h1, s) + seg_row(p8, p1, s) * c
            per_l.append(cs)
        carries.append(per_l)

    for s in range(nseg):
        rows = slice(s * seg, (s + 1) * seg)
        srows = slice(s * SEG_PITCH, s * SEG_PITCH + seg)
        for l in range(nh):
            lanes = slice(l * LANES, (l + 1) * LANES)
            y = (h_s[0, l, srows, :] + p_s[0, l, srows, :] * carries[0][l][s]
                 + h_s[1, l, srows, :] + p_s[1, l, srows, :] * carries[1][l][s])
            z_ref[rows, lanes] = (_gelu_tanh(gate_ref[rows, lanes]) * y).astype(z_ref.dtype)


def _lru_core(gu, conv_w, conv_b, lam, wa, ba, wx, bx, jl, *, batch, t):
    m = gu.shape[0]
    width = gu.shape[1] // 2
    nb = width // LRU_BLOCK
    nseg = t // LRU_BLOCK
    assert nseg == 9, "scan kernel keeps 8 + 1 time segments on sublanes"
    nh = LRU_BLOCK // LANES
    srows = nseg * SEG_PITCH
    nl = conv_w.shape[0]
    wspec = pl.BlockSpec((None, 2, None, LRU_BLOCK, LRU_BLOCK), lambda b, n: (jl, 0, n, 0, 0))
    bspec = pl.BlockSpec((None, 2, None, 1, LRU_BLOCK), lambda b, n: (jl, 0, n, 0, 0))
    return pl.pallas_call(
        functools.partial(_lru_kernel, nseg=nseg),
        grid=(batch, nb),
        in_specs=[pl.BlockSpec((t, LRU_BLOCK), lambda b, n: (b, n)),
                  pl.BlockSpec((t, LRU_BLOCK), lambda b, n: (b, nb + n)),
                  pl.BlockSpec((None, CONV_WIDTH, LRU_BLOCK), lambda b, n: (jl, 0, n)),
                  pl.BlockSpec((None, 1, LRU_BLOCK), lambda b, n: (jl, 0, n)),
                  pl.BlockSpec((None, 2, LRU_BLOCK), lambda b, n: (jl, 0, n)),
                  wspec, bspec, wspec, bspec],
        out_specs=pl.BlockSpec((t, LRU_BLOCK), lambda b, n: (b, n)),
        out_shape=jax.ShapeDtypeStruct((m, width), BF16),
        scratch_shapes=[pltpu.VMEM((nh, t + 24, LANES), F32),
                        pltpu.VMEM((LRU_BLOCK, 4 * LRU_BLOCK), BF16),
                        pltpu.VMEM((2, nh, srows, LANES), F32),
                        pltpu.VMEM((2, nh, srows, LANES), F32),
                        pltpu.VMEM((2, nh, srows, LANES), F32),
                        pltpu.VMEM((2, nh, srows, LANES), F32)],
        compiler_params=_cparams("parallel", "parallel"),
        name="lru_core",
    )(gu, gu, conv_w, conv_b.reshape(nl, 1, width), lam, wa, ba.reshape(nl, 2, nb, 1, LRU_BLOCK),
      wx, bx.reshape(nl, 2, nb, 1, LRU_BLOCK))


_NT = (((1,), (1,)), ((), ()))
_EXP2_SCALE = (HEAD_DIM ** -0.5) * math.log2(math.e)


def _softmax_rows(parts):
    m = parts[0].max(axis=-1, keepdims=True)
    for p in parts[1:]:
        m = jnp.maximum(m, p.max(axis=-1, keepdims=True))
    es = [jnp.exp2((p - m) * _EXP2_SCALE) for p in parts]
    tot = es[0].sum(axis=-1, keepdims=True)
    for e in es[1:]:
        tot = tot + e.sum(axis=-1, keepdims=True)
    inv = 1.0 / tot
    return [(e * inv).astype(BF16) for e in es]


def _softmax_tiles(tiles):
    m = functools.reduce(jnp.maximum, tiles).max(axis=-1, keepdims=True)
    es = [jnp.exp2((t - m) * _EXP2_SCALE) for t in tiles]
    inv = 1.0 / functools.reduce(jnp.add, es).sum(axis=-1, keepdims=True)
    return [e * inv for e in es]


def _na_kernel(q_ref, k_ref, v_ref, pairs_ref, o_ref, s_a, s_b, p_a, p_b, *, n_qblocks, tab, masked):
    qb = QROWS * GRID_W
    ctx_rows = slice(n_qblocks * qb, n_qblocks * qb + CTX_ROWS)
    kc = k_ref[ctx_rows, :]
    vc = v_ref[ctx_rows, :]

    if o_ref.shape[0] > n_qblocks * qb:
        s_c = lax.dot_general(q_ref[ctx_rows, :], kc, _NT, preferred_element_type=F32)
        (p_c,) = _softmax_rows([s_c])
        o_ref[ctx_rows, :] = jnp.dot(p_c, vc, preferred_element_type=F32).astype(o_ref.dtype)

    n_rows = n_qblocks * QROWS
    nloc = KROWS * GRID_W

    wide = 2 * GRID_W

    def q_rows(jj):
        return slice(jj * qb, (jj + 1) * qb)

    def window(jj):
        ws = min(max(jj * QROWS - WIN_ROWS // 2, 0), n_rows - KROWS)
        return slice(ws * GRID_W, ws * GRID_W + nloc)

    def logits(jj, s_ref):
        q = q_ref[q_rows(jj), :]
        s_ref[:, :nloc] = lax.dot_general(q, k_ref[window(jj), :], _NT, preferred_element_type=F32)
        s_ref[:, nloc:] = lax.dot_general(q, kc, _NT, preferred_element_type=F32)

    def probs(jj, s_ref, p_ref):
        var = 0 if jj == 0 else 2 if jj == n_qblocks - 1 else 1
        for c in range(qb // NA_CHUNK):
            a, off = divmod(c * NA_CHUNK, GRID_W)
            rows = slice(c * NA_CHUNK, (c + 1) * NA_CHUNK)
            live = [m for m in range(KROWS // 2) if tab[var][a][m] != masked]
            tiles = [s_ref[rows, m * wide:(m + 1) * wide] + pairs_ref[tab[var][a][m], off:off + NA_CHUNK, :]
                     for m in live]
            tiles += [s_ref[rows, nloc + i * wide:nloc + (i + 1) * wide] for i in range(CTX_ROWS // wide)]
            ps = _softmax_tiles(tiles)
            for m in range(KROWS // 2):
                p_m = ps[live.index(m)] if m in live else jnp.zeros((NA_CHUNK, wide), F32)
                p_ref[rows, m * wide:(m + 1) * wide] = p_m.astype(BF16)
            for i in range(CTX_ROWS // wide):
                p_ref[rows, nloc + i * wide:nloc + (i + 1) * wide] = ps[len(live) + i].astype(BF16)

    def attend(jj, p_ref):
        o = (jnp.dot(p_ref[:, :nloc], v_ref[window(jj), :], preferred_element_type=F32)
             + jnp.dot(p_ref[:, nloc:], vc, preferred_element_type=F32))
        o_ref[q_rows(jj), :] = o.astype(o_ref.dtype)

    bufs = ((s_a, p_a), (s_b, p_b))
    logits(0, s_a)
    for jj in range(n_qblocks):
        s_cur, p_cur = bufs[jj % 2]
        if jj + 1 < n_qblocks:
            logits(jj + 1, bufs[(jj + 1) % 2][0])
        probs(jj, s_cur, p_cur)
        attend(jj, p_cur)


def _na_window_plan(n_rows):
    n_dr = 2 * WIN_ROWS - 1
    idx = np.full((3, QROWS, KROWS), n_dr, np.int32)
    n_qblocks = n_rows // QROWS
    for v, jj in enumerate((0, 1, n_qblocks - 1)):
        r0 = jj * QROWS
        ws = int(np.clip(r0 - WIN_ROWS // 2, 0, n_rows - KROWS))
        for a in range(QROWS):
            qr = r0 + a
            rs = int(np.clip(qr - WIN_ROWS // 2, 0, n_rows - WIN_ROWS))
            for kr in range(KROWS):
                kra = ws + kr
                if rs <= kra < rs + WIN_ROWS:
                    idx[v, a, kr] = kra - qr + WIN_ROWS - 1
    pairs = sorted({(int(idx[v, a, 2 * m]), int(idx[v, a, 2 * m + 1]))
                    for v in range(3) for a in range(QROWS) for m in range(KROWS // 2)})
    tab = tuple(tuple(tuple(pairs.index((int(idx[v, a, 2 * m]), int(idx[v, a, 2 * m + 1])))
                            for m in range(KROWS // 2)) for a in range(QROWS)) for v in range(3))
    masked = pairs.index((n_dr, n_dr)) if (n_dr, n_dr) in pairs else -1
    return pairs, tab, masked


def _na_bias_pairs(rpb, pairs):
    nl, n_heads = rpb.shape[:2]
    qc = np.arange(GRID_W)[:, None]
    kc = np.arange(GRID_W)[None, :]
    cs = np.clip(qc - WIN_COLS // 2, 0, GRID_W - WIN_COLS)
    col_ok = (kc >= cs) & (kc < cs + WIN_COLS)
    col_off = np.where(col_ok, kc - qc + WIN_COLS - 1, 0)
    onehot = (np.arange(2 * WIN_COLS - 1)[:, None, None] == col_off[None]) & col_ok[None]
    onehot = jnp.asarray(onehot.astype(np.float32))
    t2 = jnp.einsum('lhrm,mqk->lhrqk', rpb.astype(F32) * (HEAD_DIM ** 0.5), onehot,
                    precision=lax.Precision.HIGHEST)
    t2 = jnp.where(jnp.asarray(col_ok)[None, None, None], t2, NEG)
    masked = jnp.full((nl, n_heads, GRID_W, GRID_W), NEG, F32)
    rows = [t2[:, :, r] for r in range(2 * WIN_ROWS - 1)] + [masked]
    tiles = [jnp.concatenate([rows[left], rows[right]], axis=-1) for left, right in pairs]
    return jnp.stack(tiles, axis=2)


def _na_attention(qkv, bias_pairs, tab, masked, jl, *, batch, t, with_ctx):
    t_out = t if with_ctx else t - CTX_ROWS
    d = qkv.shape[1] // 3
    n_heads = d // HEAD_DIM
    qb = QROWS * GRID_W
    n_qblocks = (t - CTX_ROWS) // qb
    n_pairs = bias_pairs.shape[2]
    n_keys = KROWS * GRID_W + CTX_ROWS
    assert n_qblocks % 2 == 0
    return pl.pallas_call(
        functools.partial(_na_kernel, n_qblocks=n_qblocks, tab=tab, masked=masked),
        grid=(n_heads, batch),
        in_specs=[pl.BlockSpec((t, HEAD_DIM), lambda h, b: (b, h)),
                  pl.BlockSpec((t, HEAD_DIM), lambda h, b: (b, n_heads + h)),
                  pl.BlockSpec((t, HEAD_DIM), lambda h, b: (b, 2 * n_heads + h)),
                  pl.BlockSpec((None, None, n_pairs, GRID_W, 2 * GRID_W), lambda h, b: (jl, h, 0, 0, 0))],
        out_specs=pl.BlockSpec((t_out, HEAD_DIM), lambda h, b: (b, h)),
        out_shape=jax.ShapeDtypeStruct((batch * t_out, d), BF16),
        scratch_shapes=[pltpu.VMEM((qb, n_keys), F32), pltpu.VMEM((qb, n_keys), F32),
                        pltpu.VMEM((qb, n_keys), BF16), pltpu.VMEM((qb, n_keys), BF16)],
        compiler_params=_cparams("parallel", "parallel"),
        name="na_attn",
    )(qkv, qkv, qkv, bias_pairs)


def kernel(x, c, ctx, c_ctx, ada_w, ada_b, norm1_g, norm2_g, mlp_w1, mlp_w2, lru_w_in, lru_conv_w, lru_conv_b, lru_lambda, lru_wa, lru_ba, lru_wx, lru_bx, lru_w_out, na_w_qkv, na_rpb, na_w_o, final_g):
    batch, seq, d = x.shape
    ctx_len = ctx.shape[1]
    depth = ada_w.shape[0]
    assert ctx_len == CTX_ROWS and batch <= 4 and seq % (QROWS * GRID_W) == 0
    t = ctx_len + seq
    tpb = 3
    tm = t // tpb
    tpb_mm = 2
    tm_mm = t // tpb_mm
    assert tm % CTX_ROWS == 0 and tm_mm % LANES == 0 and tm_mm >= CTX_ROWS
    tm_lat_res = min(1024, seq)
    tm_lat = min(512, seq)
    tf = min(1024, mlp_w1.shape[2])

    def col_tile(n, target):
        return max(c for c in range(LANES, target + 1, LANES) if n % c == 0)

    tn_in = col_tile(lru_w_in.shape[2], 1536)
    tn_qkv = col_tile(na_w_qkv.shape[2], 1536)
    tn_res = col_tile(d, 1024)

    xs = jnp.concatenate([x, ctx], axis=1).reshape(batch * t, d)
    fg = final_g.reshape(1, d)
    c8 = jnp.zeros((8, d), F32).at[:batch].set(c).at[4].set(c_ctx)
    mods = _ada_mods(c8, ada_w, ada_b).reshape(depth, 8, N_MOD, 1, d)
    g1 = norm1_g.reshape(depth, 1, d)
    g2 = norm2_g.reshape(depth, 1, d)
    w1, w2 = mlp_w1.astype(BF16), mlp_w2.astype(BF16)
    w_in, w_out = lru_w_in.astype(BF16), lru_w_out.astype(BF16)
    w_qkv, w_o = na_w_qkv.astype(BF16), na_w_o.astype(BF16)
    pairs, tab, masked = _na_window_plan(seq // GRID_W)
    bias_pairs = _na_bias_pairs(na_rpb, pairs)

    for i in range(depth):
        j = i // 2
        if i % 2 == 0:
            gu = _mm_norm(xs, g1, mods, i, w_in, j, F32, tm=tm_mm, tn=tn_in, tpb=tpb_mm)
            mixed = _lru_core(gu, lru_conv_w, lru_conv_b, lru_lambda, lru_wa, lru_ba, lru_wx, lru_bx, j,
                              batch=batch, t=t)
            w_mix = w_out
        else:
            qkv = _mm_norm(xs, g1, mods, i, w_qkv, j, BF16, tm=tm_mm, tn=tn_qkv, tpb=tpb_mm)
            mixed = _na_attention(qkv, bias_pairs, tab, masked, j, batch=batch, t=t,
                                  with_ctx=i < depth - 1)
            w_mix = w_o
        if i < depth - 1:
            xs = _mm_res(mixed, w_mix, j, xs, mods, i, tm=tm_mm, tn=tn_res, tpb=tpb_mm)
            xs = _mlp(xs, g2, mods, i, w1, w2, fg, tm=tm, tf=tf, tpb=tpb, ctx_rows=CTX_ROWS,
                      final_norm=False)
        else:
            x_lat = _mm_res_latent(mixed, w_mix, j, xs, mods, i, batch=batch, seq=seq, tm=tm_lat_res,
                                   tn=tn_res)
            out = _mlp(x_lat.reshape(batch * seq, d), g2, mods, i, w1, w2, fg, tm=tm_lat, tf=tf,
                       tpb=seq // tm_lat, ctx_rows=0, final_norm=True)
    return out.reshape(batch, seq, d)
```

```python
import functools
import math

import numpy as np
import jax
import jax.numpy as jnp
from jax import lax
from jax.experimental import pallas as pl
from jax.experimental.pallas import tpu as pltpu

GRID_W = 64
WIN_ROWS = 8
WIN_COLS = 16
HEAD_DIM = 128
LRU_BLOCK = 256
CONV_WIDTH = 4
LRU_C = 8.0
N_MOD = 6
EPS = 1e-6

LANES = 128
CTX_ROWS = 256
NORM_CHUNK = 32
QROWS = 4
KROWS = QROWS + WIN_ROWS
NA_CHUNK = 32
SEG_PITCH = LRU_BLOCK + 8
SCAN_UNROLL = 4
MM_ROW_STEPS = (384, 256, 128)
NEG = -1e30
VMEM_LIMIT = 56 * 1024 * 1024

F32 = jnp.float32
BF16 = jnp.bfloat16


def _cparams(*sem):
    return pltpu.CompilerParams(dimension_semantics=sem, vmem_limit_bytes=VMEM_LIMIT)


def _mod_spec(layer, k, row_fn, width, col_fn=None):
    if col_fn is None:
        return pl.BlockSpec((None, None, None, 1, width),
                            lambda *g: (layer, row_fn(*g), k, 0, 0))
    return pl.BlockSpec((None, None, None, 1, width),
                        lambda *g: (layer, row_fn(*g), k, 0, col_fn(*g)))


def _row_step(tm):
    return next(s for s in MM_ROW_STEPS if tm % s == 0)


def _gain_spec(layer, d):
    return pl.BlockSpec((None, 1, d), lambda *g: (layer, 0, 0))


def _ada_kernel(c_ref, w_ref, b_ref, o_ref):
    c = c_ref[...]
    s = (c * jax.nn.sigmoid(c)).astype(BF16)
    o_ref[...] = jnp.dot(s, w_ref[...].astype(BF16), preferred_element_type=F32) + b_ref[...]


def _ada_mods(c8, ada_w, ada_b, tn=2048):
    depth, d, n = ada_w.shape
    tn = math.gcd(tn, n)
    return pl.pallas_call(
        _ada_kernel,
        grid=(depth, n // tn),
        in_specs=[pl.BlockSpec((8, d), lambda l, j: (0, 0)),
                  pl.BlockSpec((None, d, tn), lambda l, j: (l, 0, j)),
                  pl.BlockSpec((None, 1, tn), lambda l, j: (l, 0, j))],
        out_specs=pl.BlockSpec((None, 8, tn), lambda l, j: (l, 0, j)),
        out_shape=jax.ShapeDtypeStruct((depth, 8, n), F32),
        compiler_params=_cparams("parallel", "parallel"),
        name="ada_mods",
    )(c8, ada_w, ada_b.reshape(depth, 1, n))


def _norm_mod_rows(x_ref, g_ref, sh_ref, sc_ref, csh_ref, csc_ref, last, hn_ref, rinv_ref, ctx_rows):
    tm, d = x_ref.shape

    def rms_body(r, carry):
        rows = pl.ds(pl.multiple_of(r * NORM_CHUNK, NORM_CHUNK), NORM_CHUNK)
        xs = x_ref[rows, :]
        ms = jnp.mean(xs * xs, axis=-1, keepdims=True)
        rinv_ref[rows, :] = jnp.broadcast_to(lax.rsqrt(ms + EPS), (NORM_CHUNK, LANES))
        return carry

    lax.fori_loop(0, tm // NORM_CHUNK, rms_body, 0, unroll=4)

    g = g_ref[...]
    sh = sh_ref[...]
    gain = g * (1.0 + sc_ref[...])

    def scale_rows(lo, hi, gain_row, shift_row):
        def body(r, carry):
            rows = pl.ds(pl.multiple_of(r * NORM_CHUNK, NORM_CHUNK), NORM_CHUNK)
            rinv = rinv_ref[rows, :]
            for c in range(d // LANES):
                lanes = slice(c * LANES, (c + 1) * LANES)
                hn_ref[rows, lanes] = (x_ref[rows, lanes] * rinv * gain_row[:, lanes]
                                       + shift_row[:, lanes]).astype(hn_ref.dtype)
            return carry

        lax.fori_loop(lo, hi, body, 0, unroll=2)

    n_plain = (tm - ctx_rows) // NORM_CHUNK
    scale_rows(0, n_plain, gain, sh)
    if ctx_rows:
        sh_tail = jnp.where(last, csh_ref[...], sh)
        gain_tail = jnp.where(last, g * (1.0 + csc_ref[...]), gain)
        scale_rows(n_plain, tm // NORM_CHUNK, gain_tail, sh_tail)


def _mm_norm_kernel(x_ref, g_ref, sh_ref, sc_ref, csh_ref, csc_ref, w_ref, o_ref, hn_ref, rinv_ref,
                    *, tpb):
    i = pl.program_id(0)

    @pl.when(pl.program_id(1) == 0)
    def _():
        _norm_mod_rows(x_ref, g_ref, sh_ref, sc_ref, csh_ref, csc_ref, (i % tpb) == tpb - 1, hn_ref,
                       rinv_ref, CTX_ROWS)

    tm = o_ref.shape[0]
    step = _row_step(tm)
    for r0 in range(0, tm, step):
        rows = slice(r0, r0 + step)
        o_ref[rows, :] = jnp.dot(hn_ref[rows, :], w_ref[...], preferred_element_type=F32).astype(o_ref.dtype)


def _mm_norm(x, gains, mods, layer, w, wl, out_dtype, *, tm, tn, tpb):
    m, d = x.shape
    n = w.shape[2]
    row = lambda i, j: i // tpb
    ctx = lambda i, j: 4
    return pl.pallas_call(
        functools.partial(_mm_norm_kernel, tpb=tpb),
        grid=(m // tm, n // tn),
        in_specs=[pl.BlockSpec((tm, d), lambda i, j: (i, 0)),
                  _gain_spec(layer, d),
                  _mod_spec(layer, 0, row, d), _mod_spec(layer, 1, row, d),
                  _mod_spec(layer, 0, ctx, d), _mod_spec(layer, 1, ctx, d),
                  pl.BlockSpec((None, d, tn), lambda i, j: (wl, 0, j))],
        out_specs=pl.BlockSpec((tm, tn), lambda i, j: (i, j)),
        out_shape=jax.ShapeDtypeStruct((m, n), out_dtype),
        scratch_shapes=[pltpu.VMEM((tm, d), BF16), pltpu.VMEM((tm, LANES), F32)],
        compiler_params=_cparams("parallel", "arbitrary"),
        name="mm_norm",
    )(x, gains, mods, mods, mods, mods, w)


def _mm_res_kernel(a_ref, w_ref, r_ref, gt_ref, cgt_ref, o_ref, *, tpb, ctx_rows):
    g = gt_ref[...]
    tm = o_ref.shape[0]
    split = tm - ctx_rows
    g_tail = jnp.where((pl.program_id(1) % tpb) == tpb - 1, cgt_ref[...], g) if ctx_rows else g
    step = _row_step(tm)
    for r0 in range(0, tm, step):
        acc = jnp.dot(a_ref[r0:r0 + step, :], w_ref[...], preferred_element_type=F32)
        cut = min(max(split - r0, 0), step)
        if cut:
            o_ref[r0:r0 + cut, :] = r_ref[r0:r0 + cut, :] + g * acc[:cut]
        if cut < step:
            o_ref[r0 + cut:r0 + step, :] = r_ref[r0 + cut:r0 + step, :] + g_tail * acc[cut:]


def _mm_res(a, w, wl, res, mods, layer, *, tm, tn, tpb):
    m, k = a.shape
    n = w.shape[2]
    return pl.pallas_call(
        functools.partial(_mm_res_kernel, tpb=tpb, ctx_rows=CTX_ROWS),
        grid=(n // tn, m // tm),
        in_specs=[pl.BlockSpec((tm, k), lambda j, i: (i, 0)),
                  pl.BlockSpec((None, k, tn), lambda j, i: (wl, 0, j)),
                  pl.BlockSpec((tm, tn), lambda j, i: (i, j)),
                  _mod_spec(layer, 2, lambda j, i: i // tpb, tn, lambda j, i: j),
                  _mod_spec(layer, 2, lambda j, i: 4, tn, lambda j, i: j)],
        out_specs=pl.BlockSpec((tm, tn), lambda j, i: (i, j)),
        out_shape=jax.ShapeDtypeStruct((m, n), F32),
        compiler_params=_cparams("parallel", "parallel"),
        name="mm_res",
    )(a, w, res, mods, mods)


def _mm_res_latent(a, w, wl, res, mods, layer, *, batch, seq, tm, tn):
    k = a.shape[1]
    n = w.shape[2]
    tpb = seq // tm
    tile = lambda j, i: (i // tpb, i % tpb)
    return pl.pallas_call(
        functools.partial(_mm_res_kernel, tpb=tpb, ctx_rows=0),
        grid=(n // tn, batch * tpb),
        in_specs=[pl.BlockSpec((None, tm, k), lambda j, i: (*tile(j, i), 0)),
                  pl.BlockSpec((None, k, tn), lambda j, i: (wl, 0, j)),
                  pl.BlockSpec((None, tm, tn), lambda j, i: (*tile(j, i), j)),
                  _mod_spec(layer, 2, lambda j, i: i // tpb, tn, lambda j, i: j),
                  _mod_spec(layer, 2, lambda j, i: 4, tn, lambda j, i: j)],
        out_specs=pl.BlockSpec((None, tm, tn), lambda j, i: (*tile(j, i), j)),
        out_shape=jax.ShapeDtypeStruct((batch, seq, n), F32),
        compiler_params=_cparams("parallel", "parallel"),
        name="mm_res_latent",
    )(a.reshape(batch, a.shape[0] // batch, k), w, res.reshape(batch, res.shape[0] // batch, n), mods, mods)


def _mlp_kernel(x_ref, g_ref, sh_ref, sc_ref, gt_ref, csh_ref, csc_ref, cgt_ref, w1_ref, w2_ref, fg_ref,
                o_ref, hn_ref, rinv_ref, *, tpb, ctx_rows, final_norm):
    i = pl.program_id(0)
    f = pl.program_id(1)
    last = (i % tpb) == tpb - 1

    @pl.when(f == 0)
    def _():
        _norm_mod_rows(x_ref, g_ref, sh_ref, sc_ref, csh_ref, csc_ref, last, hn_ref, rinv_ref, ctx_rows)
        o_ref[...] = jnp.zeros_like(o_ref)

    a = jnp.dot(hn_ref[...], w1_ref[...], preferred_element_type=F32)
    a = jnp.square(jnp.maximum(a, 0.0)).astype(BF16)
    o_ref[...] += jnp.dot(a, w2_ref[...], preferred_element_type=F32)

    @pl.when(f == pl.num_programs(1) - 1)
    def _():
        g = gt_ref[...]
        tm = o_ref.shape[0]
        split = tm - ctx_rows
        if final_norm:
            fg = fg_ref[...]
            d = o_ref.shape[1]

            def resid_body(r, carry):
                rows = pl.ds(pl.multiple_of(r * NORM_CHUNK, NORM_CHUNK), NORM_CHUNK)
                y = x_ref[rows, :] + g * o_ref[rows, :]
                o_ref[rows, :] = y
                ms = jnp.mean(y * y, axis=-1, keepdims=True)
                rinv_ref[rows, :] = jnp.broadcast_to(lax.rsqrt(ms + EPS), (NORM_CHUNK, LANES))
                return carry

            lax.fori_loop(0, tm // NORM_CHUNK, resid_body, 0, unroll=4)

            def norm_body(r, carry):
                rows = pl.ds(pl.multiple_of(r * NORM_CHUNK, NORM_CHUNK), NORM_CHUNK)
                rinv = rinv_ref[rows, :]
                for c in range(d // LANES):
                    lanes = slice(c * LANES, (c + 1) * LANES)
                    o_ref[rows, lanes] = o_ref[rows, lanes] * rinv * fg[:, lanes]
                return carry

            lax.fori_loop(0, tm // NORM_CHUNK, norm_body, 0, unroll=2)
        else:
            o_ref[:split, :] = x_ref[:split, :] + g * o_ref[:split, :]
            if ctx_rows:
                g_tail = jnp.where(last, cgt_ref[...], g)
                o_ref[split:, :] = x_ref[split:, :] + g_tail * o_ref[split:, :]


def _mlp(x, gains, mods, layer, w1, w2, final_g, *, tm, tf, tpb, ctx_rows, final_norm):
    assert not (final_norm and ctx_rows)
    m, d = x.shape
    dff = w1.shape[2]
    row = lambda i, f: i // tpb
    ctx = lambda i, f: 4
    return pl.pallas_call(
        functools.partial(_mlp_kernel, tpb=tpb, ctx_rows=ctx_rows, final_norm=final_norm),
        grid=(m // tm, dff // tf),
        in_specs=[pl.BlockSpec((tm, d), lambda i, f: (i, 0)),
                  _gain_spec(layer, d),
                  _mod_spec(layer, 3, row, d), _mod_spec(layer, 4, row, d), _mod_spec(layer, 5, row, d),
                  _mod_spec(layer, 3, ctx, d), _mod_spec(layer, 4, ctx, d), _mod_spec(layer, 5, ctx, d),
                  pl.BlockSpec((None, d, tf), lambda i, f: (layer, 0, f)),
                  pl.BlockSpec((None, tf, d), lambda i, f: (layer, f, 0)),
                  pl.BlockSpec((1, d), lambda i, f: (0, 0))],
        out_specs=pl.BlockSpec((tm, d), lambda i, f: (i, 0)),
        out_shape=jax.ShapeDtypeStruct((m, d), F32),
        scratch_shapes=[pltpu.VMEM((tm, d), BF16), pltpu.VMEM((tm, LANES), F32)],
        compiler_params=_cparams("parallel", "arbitrary"),
        name="mlp",
    )(x, gains, mods, mods, mods, mods, mods, mods, w1, w2, final_g)


def _gelu_tanh(x):
    c = 0.7978845608028654
    hx = 0.5 * x
    return hx + hx * jnp.tanh(x * (c + (c * 0.044715) * (x * x)))


def _lru_kernel(gate_ref, u_ref, cw_ref, cb_ref, lam_ref, wa_ref, ba_ref, wx_ref, bx_ref, z_ref,
                upad, wcat, a_s, b_s, p_s, h_s, *, nseg):
    seg = LRU_BLOCK
    nh = LRU_BLOCK // LANES
    n_lat = (nseg - 1) * seg
    ctx0 = 8 + n_lat + 8
    zeros8 = jnp.zeros((8, LANES), F32)
    for l in range(nh):
        lanes = slice(l * LANES, (l + 1) * LANES)
        upad[l, 0:8, :] = zeros8
        upad[l, 8:8 + n_lat, :] = u_ref[0:n_lat, lanes]
        upad[l, 8 + n_lat:ctx0, :] = zeros8
        upad[l, ctx0:ctx0 + CTX_ROWS, :] = u_ref[n_lat:, lanes]
        upad[l, ctx0 + CTX_ROWS:ctx0 + CTX_ROWS + 8, :] = zeros8
    for d in range(2):
        wcat[:, (2 * d) * seg:(2 * d + 1) * seg] = (0.5 * wa_ref[d]).astype(BF16)
        wcat[:, (2 * d + 1) * seg:(2 * d + 2) * seg] = (0.5 * wx_ref[d]).astype(BF16)

    nlam = -lam_ref[...]
    hcoef = (-0.5 * LRU_C) * (jnp.maximum(nlam, 0.0) + jnp.log1p(jnp.exp(-jnp.abs(nlam))))
    cw = cw_ref[...]
    cb = cb_ref[...]
    hba = 0.5 * ba_ref[...]
    hbx = 0.5 * bx_ref[...]

    def coeff_body(s, carry):
        base = s * seg + jnp.where(s == nseg - 1, 16, 8)
        ucs = []
        for l in range(nh):
            lanes = slice(l * LANES, (l + 1) * LANES)
            acc = cb[:, lanes]
            for k in range(CONV_WIDTH):
                acc = acc + cw[k:k + 1, lanes] * upad[l, pl.ds(base + (k - CONV_WIDTH // 2), seg), :]
            ucs.append(acc)
        uc = jnp.concatenate(ucs, axis=-1)
        gts = jnp.dot(uc.astype(BF16), wcat[...], preferred_element_type=F32)
        row0 = pl.multiple_of(s * SEG_PITCH, 8)
        for l in range(nh):
            lanes = slice(l * LANES, (l + 1) * LANES)
            huc = 0.5 * ucs[l]
            for d in range(2):
                ga = gts[:, 2 * d * seg + l * LANES:2 * d * seg + (l + 1) * LANES]
                gx = gts[:, (2 * d + 1) * seg + l * LANES:(2 * d + 1) * seg + (l + 1) * LANES]
                hc = hcoef[d:d + 1, lanes]
                log_a = jnp.tanh(ga + hba[d, :, lanes]) * hc + hc
                a = jnp.exp(log_a)
                gated_u = (jnp.tanh(gx + hbx[d, :, lanes]) + 1.0) * huc
                bb = jnp.sqrt(jnp.tanh(log_a) * (-1.0 - a * a)) * gated_u
                a_s[d, l, pl.ds(row0, seg), :] = a
                b_s[d, l, pl.ds(row0, seg), :] = bb
        return carry

    lax.fori_loop(0, nseg, coeff_body, 0, unroll=3)

    def ld(ref, d, l, j):
        v8 = ref[d, l, pl.ds(j, 8, stride=SEG_PITCH), :]
        v1 = ref[d, l, pl.ds(j + 8 * SEG_PITCH, 1), :]
        return v8, v1

    def st(ref, d, l, j, v8, v1):
        ref[d, l, pl.ds(j, 8, stride=SEG_PITCH), :] = v8
        ref[d, l, pl.ds(j + 8 * SEG_PITCH, 1), :] = v1

    def scan_body(t, carry):
        out = []
        for d in range(2):
            j = t if d == 0 else seg - 1 - t
            for l in range(nh):
                h8, h1, p8, p1 = carry[d * nh + l]
                a8, a1 = ld(a_s, d, l, j)
                b8, b1 = ld(b_s, d, l, j)
                h8 = a8 * h8 + b8
                h1 = a1 * h1 + b1
                p8 = a8 * p8
                p1 = a1 * p1
                st(p_s, d, l, j, p8, p1)
                st(h_s, d, l, j, h8, h1)
                out.append((h8, h1, p8, p1))
        return tuple(out)

    init = tuple((jnp.zeros((8, LANES), F32), jnp.zeros((1, LANES), F32),
                  jnp.ones((8, LANES), F32), jnp.ones((1, LANES), F32)) for _ in range(2 * nh))
    fin = lax.fori_loop(0, seg, scan_body, init, unroll=SCAN_UNROLL)

    def seg_row(v8, v1, s):
        return v1 if s == 8 else v8[s:s + 1, :]

    carries = []
    for d in range(2):
        per_l = []
        for l in range(nh):
            h8, h1, p8, p1 = fin[d * nh + l]
            latent = list(range(nseg - 1))
            order = [nseg - 1] + (latent if d == 0 else latent[::-1])
            c = jnp.zeros((1, LANES), F32)
            cs = {}
            for s in order:
                cs[s] = c
                c = seg_row(h8, h1, s) + seg_row(p8, p1, s) * c
            per_l.append(cs)
        carries.append(per_l)

    for s in range(nseg):
        rows = slice(s * seg, (s + 1) * seg)
        srows = slice(s * SEG_PITCH, s * SEG_PITCH + seg)
        for l in range(nh):
            lanes = slice(l * LANES, (l + 1) * LANES)
            y = (h_s[0, l, srows, :] + p_s[0, l, srows, :] * carries[0][l][s]
                 + h_s[1, l, srows, :] + p_s[1, l, srows, :] * carries[1][l][s])
            z_ref[rows, lanes] = (_gelu_tanh(gate_ref[rows, lanes]) * y).astype(z_ref.dtype)


def _lru_core(gu, conv_w, conv_b, lam, wa, ba, wx, bx, jl, *, batch, t):
    m = gu.shape[0]
    width = gu.shape[1] // 2
    nb = width // LRU_BLOCK
    nseg = t // LRU_BLOCK
    assert nseg == 9, "scan kernel keeps 8 + 1 time segments on sublanes"
    nh = LRU_BLOCK // LANES
    srows = nseg * SEG_PITCH
    nl = conv_w.shape[0]
    wspec = pl.BlockSpec((None, 2, None, LRU_BLOCK, LRU_BLOCK), lambda b, n: (jl, 0, n, 0, 0))
    bspec = pl.BlockSpec((None, 2, None, 1, LRU_BLOCK), lambda b, n: (jl, 0, n, 0, 0))
    return pl.pallas_call(
        functools.partial(_lru_kernel, nseg=nseg),
        grid=(batch, nb),
        in_specs=[pl.BlockSpec((t, LRU_BLOCK), lambda b, n: (b, n)),
                  pl.BlockSpec((t, LRU_BLOCK), lambda b, n: (b, nb + n)),
                  pl.BlockSpec((None, CONV_WIDTH, LRU_BLOCK), lambda b, n: (jl, 0, n)),
                  pl.BlockSpec((None, 1, LRU_BLOCK), lambda b, n: (jl, 0, n)),
                  pl.BlockSpec((None, 2, LRU_BLOCK), lambda b, n: (jl, 0, n)),
                  wspec, bspec, wspec, bspec],
        out_specs=pl.BlockSpec((t, LRU_BLOCK), lambda b, n: (b, n)),
        out_shape=jax.ShapeDtypeStruct((m, width), BF16),
        scratch_shapes=[pltpu.VMEM((nh, t + 24, LANES), F32),
                        pltpu.VMEM((LRU_BLOCK, 4 * LRU_BLOCK), BF16),
                        pltpu.VMEM((2, nh, srows, LANES), F32),
                        pltpu.VMEM((2, nh, srows, LANES), F32),
                        pltpu.VMEM((2, nh, srows, LANES), F32),
                        pltpu.VMEM((2, nh, srows, LANES), F32)],
        compiler_params=_cparams("parallel", "parallel"),
        name="lru_core",
    )(gu, gu, conv_w, conv_b.reshape(nl, 1, width), lam, wa, ba.reshape(nl, 2, nb, 1, LRU_BLOCK),
      wx, bx.reshape(nl, 2, nb, 1, LRU_BLOCK))


_NT = (((1,), (1,)), ((), ()))
_EXP2_SCALE = (HEAD_DIM ** -0.5) * math.log2(math.e)


def _softmax_rows(parts):
    m = parts[0].max(axis=-1, keepdims=True)
    for p in parts[1:]:
        m = jnp.maximum(m, p.max(axis=-1, keepdims=True))
    es = [jnp.exp2((p - m) * _EXP2_SCALE) for p in parts]
    tot = es[0].sum(axis=-1, keepdims=True)
    for e in es[1:]:
        tot = tot + e.sum(axis=-1, keepdims=True)
    inv = 1.0 / tot
    return [(e * inv).astype(BF16) for e in es]


def _softmax_tiles(tiles):
    m = functools.reduce(jnp.maximum, tiles).max(axis=-1, keepdims=True)
    es = [jnp.exp2((t - m) * _EXP2_SCALE) for t in tiles]
    inv = 1.0 / functools.reduce(jnp.add, es).sum(axis=-1, keepdims=True)
    return [e * inv for e in es]


def _na_kernel(q_ref, k_ref, v_ref, pairs_ref, o_ref, s_a, s_b, p_a, p_b, *, n_qblocks, tab, masked):
    qb = QROWS * GRID_W
    ctx_rows = slice(n_qblocks * qb, n_qblocks * qb + CTX_ROWS)
    kc = k_ref[ctx_rows, :]
    vc = v_ref[ctx_rows, :]

    if o_ref.shape[0] > n_qblocks * qb:
        s_c = lax.dot_general(q_ref[ctx_rows, :], kc, _NT, preferred_element_type=F32)
        (p_c,) = _softmax_rows([s_c])
        o_ref[ctx_rows, :] = jnp.dot(p_c, vc, preferred_element_type=F32).astype(o_ref.dtype)

    n_rows = n_qblocks * QROWS
    nloc = KROWS * GRID_W

    wide = 2 * GRID_W

    def q_rows(jj):
        return slice(jj * qb, (jj + 1) * qb)

    def window(jj):
        ws = min(max(jj * QROWS - WIN_ROWS // 2, 0), n_rows - KROWS)
        return slice(ws * GRID_W, ws * GRID_W + nloc)

    def logits(jj, s_ref):
        q = q_ref[q_rows(jj), :]
        s_ref[:, :nloc] = lax.dot_general(q, k_ref[window(jj), :], _NT, preferred_element_type=F32)
        s_ref[:, nloc:] = lax.dot_general(q, kc, _NT, preferred_element_type=F32)

    def probs(jj, s_ref, p_ref):
        var = 0 if jj == 0 else 2 if jj == n_qblocks - 1 else 1
        for c in range(qb // NA_CHUNK):
            a, off = divmod(c * NA_CHUNK, GRID_W)
            rows = slice(c * NA_CHUNK, (c + 1) * NA_CHUNK)
            live = [m for m in range(KROWS // 2) if tab[var][a][m] != masked]
            tiles = [s_ref[rows, m * wide:(m + 1) * wide] + pairs_ref[tab[var][a][m], off:off + NA_CHUNK, :]
                     for m in live]
            tiles += [s_ref[rows, nloc + i * wide:nloc + (i + 1) * wide] for i in range(CTX_ROWS // wide)]
            ps = _softmax_tiles(tiles)
            for m in range(KROWS // 2):
                p_m = ps[live.index(m)] if m in live else jnp.zeros((NA_CHUNK, wide), F32)
                p_ref[rows, m * wide:(m + 1) * wide] = p_m.astype(BF16)
            for i in range(CTX_ROWS // wide):
                p_ref[rows, nloc + i * wide:nloc + (i + 1) * wide] = ps[len(live) + i].astype(BF16)

    def attend(jj, p_ref):
        o = (jnp.dot(p_ref[:, :nloc], v_ref[window(jj), :], preferred_element_type=F32)
             + jnp.dot(p_ref[:, nloc:], vc, preferred_element_type=F32))
        o_ref[q_rows(jj), :] = o.astype(o_ref.dtype)

    bufs = ((s_a, p_a), (s_b, p_b))
    logits(0, s_a)
    for jj in range(n_qblocks):
        s_cur, p_cur = bufs[jj % 2]
        if jj + 1 < n_qblocks:
            logits(jj + 1, bufs[(jj + 1) % 2][0])
        probs(jj, s_cur, p_cur)
        attend(jj, p_cur)


def _na_window_plan(n_rows):
    n_dr = 2 * WIN_ROWS - 1
    idx = np.full((3, QROWS, KROWS), n_dr, np.int32)
    n_qblocks = n_rows // QROWS
    for v, jj in enumerate((0, 1, n_qblocks - 1)):
        r0 = jj * QROWS
        ws = int(np.clip(r0 - WIN_ROWS // 2, 0, n_rows - KROWS))
        for a in range(QROWS):
            qr = r0 + a
            rs = int(np.clip(qr - WIN_ROWS // 2, 0, n_rows - WIN_ROWS))
            for kr in range(KROWS):
                kra = ws + kr
                if rs <= kra < rs + WIN_ROWS:
                    idx[v, a, kr] = kra - qr + WIN_ROWS - 1
    pairs = sorted({(int(idx[v, a, 2 * m]), int(idx[v, a, 2 * m + 1]))
                    for v in range(3) for a in range(QROWS) for m in range(KROWS // 2)})
    tab = tuple(tuple(tuple(pairs.index((int(idx[v, a, 2 * m]), int(idx[v, a, 2 * m + 1])))
                            for m in range(KROWS // 2)) for a in range(QROWS)) for v in range(3))
    masked = pairs.index((n_dr, n_dr)) if (n_dr, n_dr) in pairs else -1
    return pairs, tab, masked


def _na_bias_pairs(rpb, pairs):
    nl, n_heads = rpb.shape[:2]
    qc = np.arange(GRID_W)[:, None]
    kc = np.arange(GRID_W)[None, :]
    cs = np.clip(qc - WIN_COLS // 2, 0, GRID_W - WIN_COLS)
    col_ok = (kc >= cs) & (kc < cs + WIN_COLS)
    col_off = np.where(col_ok, kc - qc + WIN_COLS - 1, 0)
    onehot = (np.arange(2 * WIN_COLS - 1)[:, None, None] == col_off[None]) & col_ok[None]
    onehot = jnp.asarray(onehot.astype(np.float32))
    t2 = jnp.einsum('lhrm,mqk->lhrqk', rpb.astype(F32) * (HEAD_DIM ** 0.5), onehot,
                    precision=lax.Precision.HIGHEST)
    t2 = jnp.where(jnp.asarray(col_ok)[None, None, None], t2, NEG)
    masked = jnp.full((nl, n_heads, GRID_W, GRID_W), NEG, F32)
    rows = [t2[:, :, r] for r in range(2 * WIN_ROWS - 1)] + [masked]
    tiles = [jnp.concatenate([rows[left], rows[right]], axis=-1) for left, right in pairs]
    return jnp.stack(tiles, axis=2)


def _na_attention(qkv, bias_pairs, tab, masked, jl, *, batch, t, with_ctx):
    t_out = t if with_ctx else t - CTX_ROWS
    d = qkv.shape[1] // 3
    n_heads = d // HEAD_DIM
    qb = QROWS * GRID_W
    n_qblocks = (t - CTX_ROWS) // qb
    n_pairs = bias_pairs.shape[2]
    n_keys = KROWS * GRID_W + CTX_ROWS
    assert n_qblocks % 2 == 0
    return pl.pallas_call(
        functools.partial(_na_kernel, n_qblocks=n_qblocks, tab=tab, masked=masked),
        grid=(n_heads, batch),
        in_specs=[pl.BlockSpec((t, HEAD_DIM), lambda h, b: (b, h)),
                  pl.BlockSpec((t, HEAD_DIM), lambda h, b: (b, n_heads + h)),
                  pl.BlockSpec((t, HEAD_DIM), lambda h, b: (b, 2 * n_heads + h)),
                  pl.BlockSpec((None, None, n_pairs, GRID_W, 2 * GRID_W), lambda h, b: (jl, h, 0, 0, 0))],
        out_specs=pl.BlockSpec((t_out, HEAD_DIM), lambda h, b: (b, h)),
        out_shape=jax.ShapeDtypeStruct((batch * t_out, d), BF16),
        scratch_shapes=[pltpu.VMEM((qb, n_keys), F32), pltpu.VMEM((qb, n_keys), F32),
                        pltpu.VMEM((qb, n_keys), BF16), pltpu.VMEM((qb, n_keys), BF16)],
        compiler_params=_cparams("parallel", "parallel"),
        name="na_attn",
    )(qkv, qkv, qkv, bias_pairs)


def kernel(x, c, ctx, c_ctx, ada_w, ada_b, norm1_g, norm2_g, mlp_w1, mlp_w2, lru_w_in, lru_conv_w, lru_conv_b, lru_lambda, lru_wa, lru_ba, lru_wx, lru_bx, lru_w_out, na_w_qkv, na_rpb, na_w_o, final_g):
    batch, seq, d = x.shape
    ctx_len = ctx.shape[1]
    depth = ada_w.shape[0]
    assert ctx_len == CTX_ROWS and batch <= 4 and seq % (QROWS * GRID_W) == 0
    t = ctx_len + seq
    tpb = 3
    tm = t // tpb
    tpb_mm = 2
    tm_mm = t // tpb_mm
    assert tm % CTX_ROWS == 0 and tm_mm % LANES == 0 and tm_mm >= CTX_ROWS
    tm_lat_res = min(1024, seq)
    tm_lat = min(512, seq)
    tf = min(1024, mlp_w1.shape[2])

    def col_tile(n, target):
        return max(c for c in range(LANES, target + 1, LANES) if n % c == 0)

    tn_in = col_tile(lru_w_in.shape[2], 1536)
    tn_qkv = col_tile(na_w_qkv.shape[2], 1536)
    tn_res = col_tile(d, 1024)

    xs = jnp.concatenate([x, ctx], axis=1).reshape(batch * t, d)
    fg = final_g.reshape(1, d)
    c8 = jnp.zeros((8, d), F32).at[:batch].set(c).at[4].set(c_ctx)
    mods = _ada_mods(c8, ada_w, ada_b).reshape(depth, 8, N_MOD, 1, d)
    g1 = norm1_g.reshape(depth, 1, d)
    g2 = norm2_g.reshape(depth, 1, d)
    w1, w2 = mlp_w1.astype(BF16), mlp_w2.astype(BF16)
    w_in, w_out = lru_w_in.astype(BF16), lru_w_out.astype(BF16)
    w_qkv, w_o = na_w_qkv.astype(BF16), na_w_o.astype(BF16)
    pairs, tab, masked = _na_window_plan(seq // GRID_W)
    bias_pairs = _na_bias_pairs(na_rpb, pairs)

    for i in range(depth):
        j = i // 2
        if i % 2 == 0:
            gu = _mm_norm(xs, g1, mods, i, w_in, j, F32, tm=tm_mm, tn=tn_in, tpb=tpb_mm)
            mixed = _lru_core(gu, lru_conv_w, lru_conv_b, lru_lambda, lru_wa, lru_ba, lru_wx, lru_bx, j,
                              batch=batch, t=t)
            w_mix = w_out
        else:
            qkv = _mm_norm(xs, g1, mods, i, w_qkv, j, BF16, tm=tm_mm, tn=tn_qkv, tpb=tpb_mm)
            mixed = _na_attention(qkv, bias_pairs, tab, masked, j, batch=batch, t=t,
                                  with_ctx=i < depth - 1)
            w_mix = w_o
        if i < depth - 1:
            xs = _mm_res(mixed, w_mix, j, xs, mods, i, tm=tm_mm, tn=tn_res, tpb=tpb_mm)
            xs = _mlp(xs, g2, mods, i, w1, w2, fg, tm=tm, tf=tf, tpb=tpb, ctx_rows=CTX_ROWS,
                      final_norm=False)
        else:
            x_lat = _mm_res_latent(mixed, w_mix, j, xs, mods, i, batch=batch, seq=seq, tm=tm_lat_res,
                                   tn=tn_res)
            out = _mlp(x_lat.reshape(batch * seq, d), g2, mods, i, w1, w2, fg, tm=tm_lat, tf=tf,
                       tpb=seq // tm_lat, ctx_rows=0, final_norm=True)
    return out.reshape(batch, seq, d)
```

```python
import functools
import math

import numpy as np
import jax
import jax.numpy as jnp
from jax import lax
from jax.experimental import pallas as pl
from jax.experimental.pallas import tpu as pltpu

GRID_W = 64
WIN_ROWS = 8
WIN_COLS = 16
HEAD_DIM = 128
LRU_BLOCK = 256
CONV_WIDTH = 4
LRU_C = 8.0
N_MOD = 6
EPS = 1e-6

LANES = 128
CTX_ROWS = 256
NORM_CHUNK = 32
QROWS = 4
KROWS = QROWS + WIN_ROWS
NA_CHUNK = 32
SEG_PITCH = LRU_BLOCK + 8
SCAN_UNROLL = 4
NEG = -1e30
VMEM_LIMIT = 56 * 1024 * 1024

F32 = jnp.float32
BF16 = jnp.bfloat16


def _cparams(*sem):
    return pltpu.CompilerParams(dimension_semantics=sem, vmem_limit_bytes=VMEM_LIMIT)


def _mod_spec(layer, k, row_fn, width, col_fn=None):
    if col_fn is None:
        return pl.BlockSpec((None, None, None, 1, width),
                            lambda *g: (layer, row_fn(*g), k, 0, 0))
    return pl.BlockSpec((None, None, None, 1, width),
                        lambda *g: (layer, row_fn(*g), k, 0, col_fn(*g)))


def _gain_spec(layer, d):
    return pl.BlockSpec((None, 1, d), lambda *g: (layer, 0, 0))


def _ada_kernel(c_ref, w_ref, b_ref, o_ref):
    c = c_ref[...]
    s = (c * jax.nn.sigmoid(c)).astype(BF16)
    o_ref[...] = jnp.dot(s, w_ref[...].astype(BF16), preferred_element_type=F32) + b_ref[...]


def _ada_mods(c8, ada_w, ada_b, tn=2048):
    depth, d, n = ada_w.shape
    tn = math.gcd(tn, n)
    return pl.pallas_call(
        _ada_kernel,
        grid=(depth, n // tn),
        in_specs=[pl.BlockSpec((8, d), lambda l, j: (0, 0)),
                  pl.BlockSpec((None, d, tn), lambda l, j: (l, 0, j)),
                  pl.BlockSpec((None, 1, tn), lambda l, j: (l, 0, j))],
        out_specs=pl.BlockSpec((None, 8, tn), lambda l, j: (l, 0, j)),
        out_shape=jax.ShapeDtypeStruct((depth, 8, n), F32),
        compiler_params=_cparams("parallel", "parallel"),
        name="ada_mods",
    )(c8, ada_w, ada_b.reshape(depth, 1, n))


def _norm_mod_rows(x_ref, g_ref, sh_ref, sc_ref, csh_ref, csc_ref, last, hn_ref, rinv_ref, ctx_rows):
    tm, d = x_ref.shape

    def rms_body(r, carry):
        rows = pl.ds(pl.multiple_of(r * NORM_CHUNK, NORM_CHUNK), NORM_CHUNK)
        xs = x_ref[rows, :]
        ms = jnp.mean(xs * xs, axis=-1, keepdims=True)
        rinv_ref[rows, :] = jnp.broadcast_to(lax.rsqrt(ms + EPS), (NORM_CHUNK, LANES))
        return carry

    lax.fori_loop(0, tm // NORM_CHUNK, rms_body, 0, unroll=4)

    g = g_ref[...]
    sh = sh_ref[...]
    gain = g * (1.0 + sc_ref[...])

    def scale_rows(lo, hi, gain_row, shift_row):
        def body(r, carry):
            rows = pl.ds(pl.multiple_of(r * NORM_CHUNK, NORM_CHUNK), NORM_CHUNK)
            rinv = rinv_ref[rows, :]
            for c in range(d // LANES):
                lanes = slice(c * LANES, (c + 1) * LANES)
                hn_ref[rows, lanes] = (x_ref[rows, lanes] * rinv * gain_row[:, lanes]
                                       + shift_row[:, lanes]).astype(hn_ref.dtype)
            return carry

        lax.fori_loop(lo, hi, body, 0, unroll=2)

    n_plain = (tm - ctx_rows) // NORM_CHUNK
    scale_rows(0, n_plain, gain, sh)
    if ctx_rows:
        sh_tail = jnp.where(last, csh_ref[...], sh)
        gain_tail = jnp.where(last, g * (1.0 + csc_ref[...]), gain)
        scale_rows(n_plain, tm // NORM_CHUNK, gain_tail, sh_tail)


def _mm_norm_kernel(x_ref, g_ref, sh_ref, sc_ref, csh_ref, csc_ref, w_ref, o_ref, hn_ref, rinv_ref,
                    *, tpb):
    i = pl.program_id(0)

    @pl.when(pl.program_id(1) == 0)
    def _():
        _norm_mod_rows(x_ref, g_ref, sh_ref, sc_ref, csh_ref, csc_ref, (i % tpb) == tpb - 1, hn_ref,
                       rinv_ref, CTX_ROWS)

    o_ref[...] = jnp.dot(hn_ref[...], w_ref[...], preferred_element_type=F32).astype(o_ref.dtype)


def _mm_norm(x, gains, mods, layer, w, wl, out_dtype, *, tm, tn, tpb):
    m, d = x.shape
    n = w.shape[2]
    row = lambda i, j: i // tpb
    ctx = lambda i, j: 4
    return pl.pallas_call(
        functools.partial(_mm_norm_kernel, tpb=tpb),
        grid=(m // tm, n // tn),
        in_specs=[pl.BlockSpec((tm, d), lambda i, j: (i, 0)),
                  _gain_spec(layer, d),
                  _mod_spec(layer, 0, row, d), _mod_spec(layer, 1, row, d),
                  _mod_spec(layer, 0, ctx, d), _mod_spec(layer, 1, ctx, d),
                  pl.BlockSpec((None, d, tn), lambda i, j: (wl, 0, j))],
        out_specs=pl.BlockSpec((tm, tn), lambda i, j: (i, j)),
        out_shape=jax.ShapeDtypeStruct((m, n), out_dtype),
        scratch_shapes=[pltpu.VMEM((tm, d), BF16), pltpu.VMEM((tm, LANES), F32)],
        compiler_params=_cparams("parallel", "arbitrary"),
        name="mm_norm",
    )(x, gains, mods, mods, mods, mods, w)


def _mm_res_kernel(a_ref, w_ref, r_ref, gt_ref, cgt_ref, o_ref, *, tpb, ctx_rows):
    acc = jnp.dot(a_ref[...], w_ref[...], preferred_element_type=F32)
    g = gt_ref[...]
    split = o_ref.shape[0] - ctx_rows
    o_ref[:split, :] = r_ref[:split, :] + g * acc[:split]
    if ctx_rows:
        g_tail = jnp.where((pl.program_id(1) % tpb) == tpb - 1, cgt_ref[...], g)
        o_ref[split:, :] = r_ref[split:, :] + g_tail * acc[split:]


def _mm_res(a, w, wl, res, mods, layer, *, tm, tn, tpb):
    m, k = a.shape
    n = w.shape[2]
    return pl.pallas_call(
        functools.partial(_mm_res_kernel, tpb=tpb, ctx_rows=CTX_ROWS),
        grid=(n // tn, m // tm),
        in_specs=[pl.BlockSpec((tm, k), lambda j, i: (i, 0)),
                  pl.BlockSpec((None, k, tn), lambda j, i: (wl, 0, j), pipeline_mode=pl.Buffered(1)),
                  pl.BlockSpec((tm, tn), lambda j, i: (i, j)),
                  _mod_spec(layer, 2, lambda j, i: i // tpb, tn, lambda j, i: j),
                  _mod_spec(layer, 2, lambda j, i: 4, tn, lambda j, i: j)],
        out_specs=pl.BlockSpec((tm, tn), lambda j, i: (i, j)),
        out_shape=jax.ShapeDtypeStruct((m, n), F32),
        compiler_params=_cparams("parallel", "parallel"),
        name="mm_res",
    )(a, w, res, mods, mods)


def _mm_res_latent(a, w, wl, res, mods, layer, *, batch, seq, tm, tn):
    k = a.shape[1]
    n = w.shape[2]
    tpb = seq // tm
    tile = lambda j, i: (i // tpb, i % tpb)
    return pl.pallas_call(
        functools.partial(_mm_res_kernel, tpb=tpb, ctx_rows=0),
        grid=(n // tn, batch * tpb),
        in_specs=[pl.BlockSpec((None, tm, k), lambda j, i: (*tile(j, i), 0)),
                  pl.BlockSpec((None, k, tn), lambda j, i: (wl, 0, j), pipeline_mode=pl.Buffered(1)),
                  pl.BlockSpec((None, tm, tn), lambda j, i: (*tile(j, i), j)),
                  _mod_spec(layer, 2, lambda j, i: i // tpb, tn, lambda j, i: j),
                  _mod_spec(layer, 2, lambda j, i: 4, tn, lambda j, i: j)],
        out_specs=pl.BlockSpec((None, tm, tn), lambda j, i: (*tile(j, i), j)),
        out_shape=jax.ShapeDtypeStruct((batch, seq, n), F32),
        compiler_params=_cparams("parallel", "parallel"),
        name="mm_res_latent",
    )(a.reshape(batch, a.shape[0] // batch, k), w, res.reshape(batch, res.shape[0] // batch, n), mods, mods)


def _mlp_kernel(x_ref, g_ref, sh_ref, sc_ref, gt_ref, csh_ref, csc_ref, cgt_ref, w1_ref, w2_ref, fg_ref,
                o_ref, hn_ref, rinv_ref, *, tpb, ctx_rows, final_norm):
    i = pl.program_id(0)
    f = pl.program_id(1)
    last = (i % tpb) == tpb - 1

    @pl.when(f == 0)
    def _():
        _norm_mod_rows(x_ref, g_ref, sh_ref, sc_ref, csh_ref, csc_ref, last, hn_ref, rinv_ref, ctx_rows)
        o_ref[...] = jnp.zeros_like(o_ref)

    a = jnp.dot(hn_ref[...], w1_ref[...], preferred_element_type=F32)
    a = jnp.square(jnp.maximum(a, 0.0)).astype(BF16)
    o_ref[...] += jnp.dot(a, w2_ref[...], preferred_element_type=F32)

    @pl.when(f == pl.num_programs(1) - 1)
    def _():
        g = gt_ref[...]
        tm = o_ref.shape[0]
        split = tm - ctx_rows
        if final_norm:
            fg = fg_ref[...]
            d = o_ref.shape[1]

            def resid_body(r, carry):
                rows = pl.ds(pl.multiple_of(r * NORM_CHUNK, NORM_CHUNK), NORM_CHUNK)
                y = x_ref[rows, :] + g * o_ref[rows, :]
                o_ref[rows, :] = y
                ms = jnp.mean(y * y, axis=-1, keepdims=True)
                rinv_ref[rows, :] = jnp.broadcast_to(lax.rsqrt(ms + EPS), (NORM_CHUNK, LANES))
                return carry

            lax.fori_loop(0, tm // NORM_CHUNK, resid_body, 0, unroll=4)

            def norm_body(r, carry):
                rows = pl.ds(pl.multiple_of(r * NORM_CHUNK, NORM_CHUNK), NORM_CHUNK)
                rinv = rinv_ref[rows, :]
                for c in range(d // LANES):
                    lanes = slice(c * LANES, (c + 1) * LANES)
                    o_ref[rows, lanes] = o_ref[rows, lanes] * rinv * fg[:, lanes]
                return carry

            lax.fori_loop(0, tm // NORM_CHUNK, norm_body, 0, unroll=2)
        else:
            o_ref[:split, :] = x_ref[:split, :] + g * o_ref[:split, :]
            if ctx_rows:
                g_tail = jnp.where(last, cgt_ref[...], g)
                o_ref[split:, :] = x_ref[split:, :] + g_tail * o_ref[split:, :]


def _mlp(x, gains, mods, layer, w1, w2, final_g, *, tm, tf, tpb, ctx_rows, final_norm):
    assert not (final_norm and ctx_rows)
    m, d = x.shape
    dff = w1.shape[2]
    row = lambda i, f: i // tpb
    ctx = lambda i, f: 4
    return pl.pallas_call(
        functools.partial(_mlp_kernel, tpb=tpb, ctx_rows=ctx_rows, final_norm=final_norm),
        grid=(m // tm, dff // tf),
        in_specs=[pl.BlockSpec((tm, d), lambda i, f: (i, 0)),
                  _gain_spec(layer, d),
                  _mod_spec(layer, 3, row, d), _mod_spec(layer, 4, row, d), _mod_spec(layer, 5, row, d),
                  _mod_spec(layer, 3, ctx, d), _mod_spec(layer, 4, ctx, d), _mod_spec(layer, 5, ctx, d),
                  pl.BlockSpec((None, d, tf), lambda i, f: (layer, 0, f)),
                  pl.BlockSpec((None, tf, d), lambda i, f: (layer, f, 0)),
                  pl.BlockSpec((1, d), lambda i, f: (0, 0))],
        out_specs=pl.BlockSpec((tm, d), lambda i, f: (i, 0)),
        out_shape=jax.ShapeDtypeStruct((m, d), F32),
        scratch_shapes=[pltpu.VMEM((tm, d), BF16), pltpu.VMEM((tm, LANES), F32)],
        compiler_params=_cparams("parallel", "arbitrary"),
        name="mlp",
    )(x, gains, mods, mods, mods, mods, mods, mods, w1, w2, final_g)


def _gelu_tanh(x):
    c = 0.7978845608028654
    hx = 0.5 * x
    return hx + hx * jnp.tanh(x * (c + (c * 0.044715) * (x * x)))


def _lru_kernel(gate_ref, u_ref, cw_ref, cb_ref, lam_ref, wa_ref, ba_ref, wx_ref, bx_ref, z_ref,
                upad, wcat, a_s, b_s, p_s, h_s, *, nseg):
    seg = LRU_BLOCK
    nh = LRU_BLOCK // LANES
    n_lat = (nseg - 1) * seg
    ctx0 = 8 + n_lat + 8
    zeros8 = jnp.zeros((8, LANES), F32)
    for l in range(nh):
        lanes = slice(l * LANES, (l + 1) * LANES)
        upad[l, 0:8, :] = zeros8
        upad[l, 8:8 + n_lat, :] = u_ref[0:n_lat, lanes]
        upad[l, 8 + n_lat:ctx0, :] = zeros8
        upad[l, ctx0:ctx0 + CTX_ROWS, :] = u_ref[n_lat:, lanes]
        upad[l, ctx0 + CTX_ROWS:ctx0 + CTX_ROWS + 8, :] = zeros8
    for d in range(2):
        wcat[:, (2 * d) * seg:(2 * d + 1) * seg] = (0.5 * wa_ref[d]).astype(BF16)
        wcat[:, (2 * d + 1) * seg:(2 * d + 2) * seg] = (0.5 * wx_ref[d]).astype(BF16)

    nlam = -lam_ref[...]
    hcoef = (-0.5 * LRU_C) * (jnp.maximum(nlam, 0.0) + jnp.log1p(jnp.exp(-jnp.abs(nlam))))
    cw = cw_ref[...]
    cb = cb_ref[...]
    hba = 0.5 * ba_ref[...]
    hbx = 0.5 * bx_ref[...]

    def coeff_body(s, carry):
        base = s * seg + jnp.where(s == nseg - 1, 16, 8)
        ucs = []
        for l in range(nh):
            lanes = slice(l * LANES, (l + 1) * LANES)
            acc = cb[:, lanes]
            for k in range(CONV_WIDTH):
                acc = acc + cw[k:k + 1, lanes] * upad[l, pl.ds(base + (k - CONV_WIDTH // 2), seg), :]
            ucs.append(acc)
        uc = jnp.concatenate(ucs, axis=-1)
        gts = jnp.dot(uc.astype(BF16), wcat[...], preferred_element_type=F32)
        row0 = pl.multiple_of(s * SEG_PITCH, 8)
        for l in range(nh):
            lanes = slice(l * LANES, (l + 1) * LANES)
            huc = 0.5 * ucs[l]
            for d in range(2):
                ga = gts[:, 2 * d * seg + l * LANES:2 * d * seg + (l + 1) * LANES]
                gx = gts[:, (2 * d + 1) * seg + l * LANES:(2 * d + 1) * seg + (l + 1) * LANES]
                hc = hcoef[d:d + 1, lanes]
                log_a = jnp.tanh(ga + hba[d, :, lanes]) * hc + hc
                a = jnp.exp(log_a)
                gated_u = (jnp.tanh(gx + hbx[d, :, lanes]) + 1.0) * huc
                bb = jnp.sqrt(jnp.tanh(log_a) * (-1.0 - a * a)) * gated_u
                a_s[d, l, pl.ds(row0, seg), :] = a
                b_s[d, l, pl.ds(row0, seg), :] = bb
        return carry

    lax.fori_loop(0, nseg, coeff_body, 0, unroll=3)

    def ld(ref, d, l, j):
        v8 = ref[d, l, pl.ds(j, 8, stride=SEG_PITCH), :]
        v1 = ref[d, l, pl.ds(j + 8 * SEG_PITCH, 1), :]
        return v8, v1

    def st(ref, d, l, j, v8, v1):
        ref[d, l, pl.ds(j, 8, stride=SEG_PITCH), :] = v8
        ref[d, l, pl.ds(j + 8 * SEG_PITCH, 1), :] = v1

    def scan_body(t, carry):
        out = []
        for d in range(2):
            j = t if d == 0 else seg - 1 - t
            for l in range(nh):
                h8, h1, p8, p1 = carry[d * nh + l]
                a8, a1 = ld(a_s, d, l, j)
                b8, b1 = ld(b_s, d, l, j)
                h8 = a8 * h8 + b8
                h1 = a1 * h1 + b1
                p8 = a8 * p8
                p1 = a1 * p1
                st(p_s, d, l, j, p8, p1)
                st(h_s, d, l, j, h8, h1)
                out.append((h8, h1, p8, p1))
        return tuple(out)

    init = tuple((jnp.zeros((8, LANES), F32), jnp.zeros((1, LANES), F32),
                  jnp.ones((8, LANES), F32), jnp.ones((1, LANES), F32)) for _ in range(2 * nh))
    fin = lax.fori_loop(0, seg, scan_body, init, unroll=SCAN_UNROLL)

    def seg_row(v8, v1, s):
        return v1 if s == 8 else v8[s:s + 1, :]

    carries = []
    for d in range(2):
        per_l = []
        for l in range(nh):
            h8, h1, p8, p1 = fin[d * nh + l]
            latent = list(range(nseg - 1))
            order = [nseg - 1] + (latent if d == 0 else latent[::-1])
            c = jnp.zeros((1, LANES), F32)
            cs = {}
            for s in order:
                cs[s] = c
                c = seg_row(h8, h1, s) + seg_row(p8, p1, s) * c
            per_l.append(cs)
        carries.append(per_l)

    for s in range(nseg):
        rows = slice(s * seg, (s + 1) * seg)
        srows = slice(s * SEG_PITCH, s * SEG_PITCH + seg)
        for l in range(nh):
            lanes = slice(l * LANES, (l + 1) * LANES)
            y = (h_s[0, l, srows, :] + p_s[0, l, srows, :] * carries[0][l][s]
                 + h_s[1, l, srows, :] + p_s[1, l, srows, :] * carries[1][l][s])
            z_ref[rows, lanes] = (_gelu_tanh(gate_ref[rows, lanes]) * y).astype(z_ref.dtype)


def _lru_core(gu, conv_w, conv_b, lam, wa, ba, wx, bx, jl, *, batch, t):
    m = gu.shape[0]
    width = gu.shape[1] // 2
    nb = width // LRU_BLOCK
    nseg = t // LRU_BLOCK
    assert nseg == 9, "scan kernel keeps 8 + 1 time segments on sublanes"
    nh = LRU_BLOCK // LANES
    srows = nseg * SEG_PITCH
    nl = conv_w.shape[0]
    wspec = pl.BlockSpec((None, 2, None, LRU_BLOCK, LRU_BLOCK), lambda b, n: (jl, 0, n, 0, 0))
    bspec = pl.BlockSpec((None, 2, None, 1, LRU_BLOCK), lambda b, n: (jl, 0, n, 0, 0))
    return pl.pallas_call(
        functools.partial(_lru_kernel, nseg=nseg),
        grid=(batch, nb),
        in_specs=[pl.BlockSpec((t, LRU_BLOCK), lambda b, n: (b, n)),
                  pl.BlockSpec((t, LRU_BLOCK), lambda b, n: (b, nb + n)),
                  pl.BlockSpec((None, CONV_WIDTH, LRU_BLOCK), lambda b, n: (jl, 0, n)),
                  pl.BlockSpec((None, 1, LRU_BLOCK), lambda b, n: (jl, 0, n)),
                  pl.BlockSpec((None, 2, LRU_BLOCK), lambda b, n: (jl, 0, n)),
                  wspec, bspec, wspec, bspec],
        out_specs=pl.BlockSpec((t, LRU_BLOCK), lambda b, n: (b, n)),
        out_shape=jax.ShapeDtypeStruct((m, width), BF16),
        scratch_shapes=[pltpu.VMEM((nh, t + 24, LANES), F32),
                        pltpu.VMEM((LRU_BLOCK, 4 * LRU_BLOCK), BF16),
                        pltpu.VMEM((2, nh, srows, LANES), F32),
                        pltpu.VMEM((2, nh, srows, LANES), F32),
                        pltpu.VMEM((2, nh, srows, LANES), F32),
                        pltpu.VMEM((2, nh, srows, LANES), F32)],
        compiler_params=_cparams("parallel", "parallel"),
        name="lru_core",
    )(gu, gu, conv_w, conv_b.reshape(nl, 1, width), lam, wa, ba.reshape(nl, 2, nb, 1, LRU_BLOCK),
      wx, bx.reshape(nl, 2, nb, 1, LRU_BLOCK))


_NT = (((1,), (1,)), ((), ()))
_EXP2_SCALE = (HEAD_DIM ** -0.5) * math.log2(math.e)


def _softmax_rows(parts):
    m = parts[0].max(axis=-1, keepdims=True)
    for p in parts[1:]:
        m = jnp.maximum(m, p.max(axis=-1, keepdims=True))
    es = [jnp.exp2((p - m) * _EXP2_SCALE) for p in parts]
    tot = es[0].sum(axis=-1, keepdims=True)
    for e in es[1:]:
        tot = tot + e.sum(axis=-1, keepdims=True)
    inv = 1.0 / tot
    return [(e * inv).astype(BF16) for e in es]


def _softmax_tiles(tiles):
    m = functools.reduce(jnp.maximum, tiles).max(axis=-1, keepdims=True)
    es = [jnp.exp2((t - m) * _EXP2_SCALE) for t in tiles]
    inv = 1.0 / functools.reduce(jnp.add, es).sum(axis=-1, keepdims=True)
    return [e * inv for e in es]


def _na_kernel(q_ref, k_ref, v_ref, pairs_ref, o_ref, s_a, s_b, p_a, p_b, *, n_qblocks, tab, masked):
    qb = QROWS * GRID_W
    ctx_rows = slice(n_qblocks * qb, n_qblocks * qb + CTX_ROWS)
    kc = k_ref[ctx_rows, :]
    vc = v_ref[ctx_rows, :]

    if o_ref.shape[0] > n_qblocks * qb:
        s_c = lax.dot_general(q_ref[ctx_rows, :], kc, _NT, preferred_element_type=F32)
        (p_c,) = _softmax_rows([s_c])
        o_ref[ctx_rows, :] = jnp.dot(p_c, vc, preferred_element_type=F32).astype(o_ref.dtype)

    n_rows = n_qblocks * QROWS
    nloc = KROWS * GRID_W

    wide = 2 * GRID_W

    def q_rows(jj):
        return slice(jj * qb, (jj + 1) * qb)

    def window(jj):
        ws = min(max(jj * QROWS - WIN_ROWS // 2, 0), n_rows - KROWS)
        return slice(ws * GRID_W, ws * GRID_W + nloc)

    def logits(jj, s_ref):
        q = q_ref[q_rows(jj), :]
        s_ref[:, :nloc] = lax.dot_general(q, k_ref[window(jj), :], _NT, preferred_element_type=F32)
        s_ref[:, nloc:] = lax.dot_general(q, kc, _NT, preferred_element_type=F32)

    def probs(jj, s_ref, p_ref):
        var = 0 if jj == 0 else 2 if jj == n_qblocks - 1 else 1
        for c in range(qb // NA_CHUNK):
            a, off = divmod(c * NA_CHUNK, GRID_W)
            rows = slice(c * NA_CHUNK, (c + 1) * NA_CHUNK)
            live = [m for m in range(KROWS // 2) if tab[var][a][m] != masked]
            tiles = [s_ref[rows, m * wide:(m + 1) * wide] + pairs_ref[tab[var][a][m], off:off + NA_CHUNK, :]
                     for m in live]
            tiles += [s_ref[rows, nloc + i * wide:nloc + (i + 1) * wide] for i in range(CTX_ROWS // wide)]
            ps = _softmax_tiles(tiles)
            for m in range(KROWS // 2):
                p_m = ps[live.index(m)] if m in live else jnp.zeros((NA_CHUNK, wide), F32)
                p_ref[rows, m * wide:(m + 1) * wide] = p_m.astype(BF16)
            for i in range(CTX_ROWS // wide):
                p_ref[rows, nloc + i * wide:nloc + (i + 1) * wide] = ps[len(live) + i].astype(BF16)

    def attend(jj, p_ref):
        o = (jnp.dot(p_ref[:, :nloc], v_ref[window(jj), :], preferred_element_type=F32)
             + jnp.dot(p_ref[:, nloc:], vc, preferred_element_type=F32))
        o_ref[q_rows(jj), :] = o.astype(o_ref.dtype)

    bufs = ((s_a, p_a), (s_b, p_b))
    logits(0, s_a)
    for jj in range(n_qblocks):
        s_cur, p_cur = bufs[jj % 2]
        if jj + 1 < n_qblocks:
            logits(jj + 1, bufs[(jj + 1) % 2][0])
        probs(jj, s_cur, p_cur)
        attend(jj, p_cur)


def _na_window_plan(n_rows):
    n_dr = 2 * WIN_ROWS - 1
    idx = np.full((3, QROWS, KROWS), n_dr, np.int32)
    n_qblocks = n_rows // QROWS
    for v, jj in enumerate((0, 1, n_qblocks - 1)):
        r0 = jj * QROWS
        ws = int(np.clip(r0 - WIN_ROWS // 2, 0, n_rows - KROWS))
        for a in range(QROWS):
            qr = r0 + a
            rs = int(np.clip(qr - WIN_ROWS // 2, 0, n_rows - WIN_ROWS))
            for kr in range(KROWS):
                kra = ws + kr
                if rs <= kra < rs + WIN_ROWS:
                    idx[v, a, kr] = kra - qr + WIN_ROWS - 1
    pairs = sorted({(int(idx[v, a, 2 * m]), int(idx[v, a, 2 * m + 1]))
                    for v in range(3) for a in range(QROWS) for m in range(KROWS // 2)})
    tab = tuple(tuple(tuple(pairs.index((int(idx[v, a, 2 * m]), int(idx[v, a, 2 * m + 1])))
                            for m in range(KROWS // 2)) for a in range(QROWS)) for v in range(3))
    masked = pairs.index((n_dr, n_dr)) if (n_dr, n_dr) in pairs else -1
    return pairs, tab, masked


def _na_bias_pairs(rpb, pairs):
    nl, n_heads = rpb.shape[:2]
    qc = np.arange(GRID_W)[:, None]
    kc = np.arange(GRID_W)[None, :]
    cs = np.clip(qc - WIN_COLS // 2, 0, GRID_W - WIN_COLS)
    col_ok = (kc >= cs) & (kc < cs + WIN_COLS)
    col_off = np.where(col_ok, kc - qc + WIN_COLS - 1, 0)
    onehot = (np.arange(2 * WIN_COLS - 1)[:, None, None] == col_off[None]) & col_ok[None]
    onehot = jnp.asarray(onehot.astype(np.float32))
    t2 = jnp.einsum('lhrm,mqk->lhrqk', rpb.astype(F32) * (HEAD_DIM ** 0.5), onehot,
                    precision=lax.Precision.HIGHEST)
    t2 = jnp.where(jnp.asarray(col_ok)[None, None, None], t2, NEG)
    masked = jnp.full((nl, n_heads, GRID_W, GRID_W), NEG, F32)
    rows = [t2[:, :, r] for r in range(2 * WIN_ROWS - 1)] + [masked]
    tiles = [jnp.concatenate([rows[left], rows[right]], axis=-1) for left, right in pairs]
    return jnp.stack(tiles, axis=2)


def _na_attention(qkv, bias_pairs, tab, masked, jl, *, batch, t, with_ctx):
    t_out = t if with_ctx else t - CTX_ROWS
    d = qkv.shape[1] // 3
    n_heads = d // HEAD_DIM
    qb = QROWS * GRID_W
    n_qblocks = (t - CTX_ROWS) // qb
    n_pairs = bias_pairs.shape[2]
    n_keys = KROWS * GRID_W + CTX_ROWS
    assert n_qblocks % 2 == 0
    return pl.pallas_call(
        functools.partial(_na_kernel, n_qblocks=n_qblocks, tab=tab, masked=masked),
        grid=(n_heads, batch),
        in_specs=[pl.BlockSpec((t, HEAD_DIM), lambda h, b: (b, h)),
                  pl.BlockSpec((t, HEAD_DIM), lambda h, b: (b, n_heads + h)),
                  pl.BlockSpec((t, HEAD_DIM), lambda h, b: (b, 2 * n_heads + h)),
                  pl.BlockSpec((None, None, n_pairs, GRID_W, 2 * GRID_W), lambda h, b: (jl, h, 0, 0, 0))],
        out_specs=pl.BlockSpec((t_out, HEAD_DIM), lambda h, b: (b, h)),
        out_shape=jax.ShapeDtypeStruct((batch * t_out, d), BF16),
        scratch_shapes=[pltpu.VMEM((qb, n_keys), F32), pltpu.VMEM((qb, n_keys), F32),
                        pltpu.VMEM((qb, n_keys), BF16), pltpu.VMEM((qb, n_keys), BF16)],
        compiler_params=_cparams("parallel", "parallel"),
        name="na_attn",
    )(qkv, qkv, qkv, bias_pairs)


def kernel(x, c, ctx, c_ctx, ada_w, ada_b, norm1_g, norm2_g, mlp_w1, mlp_w2, lru_w_in, lru_conv_w, lru_conv_b, lru_lambda, lru_wa, lru_ba, lru_wx, lru_bx, lru_w_out, na_w_qkv, na_rpb, na_w_o, final_g):
    batch, seq, d = x.shape
    ctx_len = ctx.shape[1]
    depth = ada_w.shape[0]
    assert ctx_len == CTX_ROWS and batch <= 4 and seq % (QROWS * GRID_W) == 0
    t = ctx_len + seq
    tpb = 3
    tm = t // tpb
    tpb_mm = 2
    tm_mm = t // tpb_mm
    assert tm % CTX_ROWS == 0 and tm_mm % LANES == 0 and tm_mm >= CTX_ROWS
    tm_lat_res = min(512, seq)
    tm_lat = min(512, seq)
    tf = min(1024, mlp_w1.shape[2])

    def col_tile(n, target):
        return max(c for c in range(LANES, target + 1, LANES) if n % c == 0)

    tn_in = col_tile(lru_w_in.shape[2], 1536)
    tn_qkv = col_tile(na_w_qkv.shape[2], 1536)
    tn_res = d

    xs = jnp.concatenate([x, ctx], axis=1).reshape(batch * t, d)
    fg = final_g.reshape(1, d)
    c8 = jnp.zeros((8, d), F32).at[:batch].set(c).at[4].set(c_ctx)
    mods = _ada_mods(c8, ada_w, ada_b).reshape(depth, 8, N_MOD, 1, d)
    g1 = norm1_g.reshape(depth, 1, d)
    g2 = norm2_g.reshape(depth, 1, d)
    w1, w2 = mlp_w1.astype(BF16), mlp_w2.astype(BF16)
    w_in, w_out = lru_w_in.astype(BF16), lru_w_out.astype(BF16)
    w_qkv, w_o = na_w_qkv.astype(BF16), na_w_o.astype(BF16)
    pairs, tab, masked = _na_window_plan(seq // GRID_W)
    bias_pairs = _na_bias_pairs(na_rpb, pairs)

    for i in range(depth):
        j = i // 2
        if i % 2 == 0:
            gu = _mm_norm(xs, g1, mods, i, w_in, j, F32, tm=tm_mm, tn=tn_in, tpb=tpb_mm)
            mixed = _lru_core(gu, lru_conv_w, lru_conv_b, lru_lambda, lru_wa, lru_ba, lru_wx, lru_bx, j,
                              batch=batch, t=t)
            w_mix = w_out
        else:
            qkv = _mm_norm(xs, g1, mods, i, w_qkv, j, BF16, tm=tm_mm, tn=tn_qkv, tpb=tpb_mm)
            mixed = _na_attention(qkv, bias_pairs, tab, masked, j, batch=batch, t=t,
                                  with_ctx=i < depth - 1)
            w_mix = w_o
        if i < depth - 1:
            xs = _mm_res(mixed, w_mix, j, xs, mods, i, tm=tm, tn=tn_res, tpb=tpb)
            xs = _mlp(xs, g2, mods, i, w1, w2, fg, tm=tm, tf=tf, tpb=tpb, ctx_rows=CTX_ROWS,
                      final_norm=False)
        else:
            x_lat = _mm_res_latent(mixed, w_mix, j, xs, mods, i, batch=batch, seq=seq, tm=tm_lat_res,
                                   tn=tn_res)
            out = _mlp(x_lat.reshape(batch * seq, d), g2, mods, i, w1, w2, fg, tm=tm_lat, tf=tf,
                       tpb=seq // tm_lat, ctx_rows=0, final_norm=True)
    return out.reshape(batch, seq, d)
```
